```python
import math
import jax
import jax.numpy as jnp
from jax import lax
import numpy as np

D_MODEL = 1024
BATCH = 4
SEQ = 8192
DEPTH = 2
DEC_BATCH = 16
DEC_SEQ = 4096
PAST_LEN = 128

HY_WIDTH = 768
FN_GROUPS = 4
FN_GROUP_DIM = 64
FN_WIDTH = FN_GROUPS * FN_GROUP_DIM
N_BRANCH = 2
IN_WIDTH = 3 * HY_WIDTH + FN_WIDTH + N_BRANCH * D_MODEL
SHORT_CONV = 3
FILTER_EMB = 33
FILTER_BANDS = (FILTER_EMB - 1) // 2
FILTER_HIDDEN = 64
MOD_SHIFT = 0.05
DECAY_TARGET = 1e-2
FAST_DECAY_PCT = 0.3
SLOW_DECAY_PCT = 1.5
N_EXPERTS = 32
TOP_K = 4
D_FF = 1024
SWIGLU_LIMIT = 7.0
SWIGLU_ALPHA = 1.702
MOE_BLOCK = 512
LN_EPS = 1e-5
DEEPNORM_ALPHA = (2 * DEPTH) ** 0.25
DEEPNORM_BETA = (8 * DEPTH) ** -0.25

kernel_name = "hyena_fnet_moe_deepnorm_encoder"

F32 = jnp.float32


def layer_norm(x, g, b):
    xf = x.astype(F32)
    mu = jnp.mean(xf, axis=-1, keepdims=True)
    xc = xf - mu
    var = jnp.mean(xc * xc, axis=-1, keepdims=True)
    return (xc * lax.rsqrt(var + LN_EPS) * g.astype(F32) + b.astype(F32)).astype(x.dtype)


def short_conv(u, w, b):
    L = u.shape[1]
    up = jnp.pad(u, ((0, 0), (1, 1), (0, 0)))
    return up[:, :L] * w[0] + up[:, 1:L + 1] * w[1] + up[:, 2:] * w[2] + b


def hyena_filters(L, w1, b1, f1, w2, b2, f2, w3, decay):
    t = jnp.linspace(0.0, 1.0, L, dtype=F32)[:, None]
    ang = 2.0 * math.pi * jnp.arange(L, dtype=F32)[:, None] / L
    freqs = jnp.linspace(1e-4, FILTER_BANDS - 1, FILTER_BANDS, dtype=F32)[None, :]
    emb = jnp.concatenate([t, jnp.cos(freqs * ang), -jnp.sin(freqs * ang)], axis=-1)
    h = jnp.sin(f1.astype(F32) * (emb @ w1.astype(F32) + b1.astype(F32)))
    h = jnp.sin(f2.astype(F32) * (h @ w2.astype(F32) + b2.astype(F32)))
    h = h @ w3.astype(F32)
    h = h * (jnp.exp(-t * jnp.abs(decay.astype(F32))) + MOD_SHIFT)
    return h[:, :HY_WIDTH], h[:, HY_WIDTH:]


def bidir_fftconv(u, k_fwd, k_bwd):
    L = u.shape[1]
    n = 2 * L
    k = jnp.concatenate([k_fwd, jnp.zeros_like(k_fwd[:1]), k_bwd[:0:-1]], axis=0)
    kf = jnp.fft.rfft(k, n=n, axis=0)
    uf = jnp.fft.rfft(u, n=n, axis=1)
    return jnp.fft.irfft(uf * kf[None], n=n, axis=1)[:, :L]


def token_mixer(h, w_in, conv_w, conv_b, filt_w1, filt_b1, filt_freq1, filt_w2, filt_b2,
                filt_freq2, filt_w3, filt_decay, hy_bias, w_hy_out, w_fn_out, w_o):
    B, L, D = h.shape
    z = h @ w_in
    hy = short_conv(z[..., :3 * HY_WIDTH], conv_w, conv_b)
    x0, x1, v = jnp.split(hy, 3, axis=-1)
    k_f, k_b = hyena_filters(L, filt_w1, filt_b1, filt_freq1, filt_w2, filt_b2, filt_freq2,
                             filt_w3, filt_decay)
    vv = (v * x1).astype(F32)
    conv = (bidir_fftconv(vv, k_f, k_b) + vv * hy_bias.astype(F32)).astype(h.dtype)
    y_hy = x0 * conv
    u_fn = z[..., 3 * HY_WIDTH:3 * HY_WIDTH + FN_WIDTH].astype(F32)
    u_fn = u_fn.reshape(B, L, FN_GROUPS, FN_GROUP_DIM)
    y_fn = jnp.real(jnp.fft.fft2(u_fn, axes=(1, 3), norm="ortho"))
    y_fn = y_fn.reshape(B, L, FN_WIDTH).astype(h.dtype)
    gates = jax.nn.sigmoid(z[..., 3 * HY_WIDTH + FN_WIDTH:].astype(F32)).astype(h.dtype)
    g_hy, g_fn = gates[..., :D], gates[..., D:]
    merged = g_hy * (y_hy @ w_hy_out) + g_fn * (y_fn @ w_fn_out)
    return merged @ w_o


def moe(h, w_router, b_router, w_gu, b_gu, w_down, b_down):
    B, L, D = h.shape
    T = B * L
    xf = h.reshape(T, D)
    logits = (xf @ w_router + b_router).astype(F32)
    top_val, top_idx = lax.top_k(logits, TOP_K)
    top_w = jax.nn.softmax(top_val, axis=-1)
    A = T * TOP_K
    e_flat = top_idx.reshape(A).astype(jnp.int32)
    tok_flat = jnp.arange(A, dtype=jnp.int32) // TOP_K
    w_flat = top_w.reshape(A)
    order = jnp.argsort(e_flat)
    e_sorted = e_flat[order]
    counts = jnp.bincount(e_flat, length=N_EXPERTS).astype(jnp.int32)
    starts = jnp.cumsum(counts) - counts
    padded = (counts + MOE_BLOCK - 1) // MOE_BLOCK * MOE_BLOCK
    pad_ends = jnp.cumsum(padded)
    pad_starts = pad_ends - padded
    dest = pad_starts[e_sorted] + (jnp.arange(A, dtype=jnp.int32) - starts[e_sorted])
    n_blocks = -(-A // MOE_BLOCK) + N_EXPERTS
    P = n_blocks * MOE_BLOCK
    slot_tok = jnp.zeros((P,), jnp.int32).at[dest].set(tok_flat[order])
    slot_w = jnp.zeros((P,), F32).at[dest].set(w_flat[order])
    block_start = jnp.arange(n_blocks, dtype=jnp.int32) * MOE_BLOCK
    block_e = jnp.minimum(jnp.searchsorted(pad_ends, block_start, side="right"),
                          N_EXPERTS - 1).astype(jnp.int32)

    def run_block(args):
        tok, wt, e = args
        xb = xf[tok]
        gu = xb @ w_gu[e] + b_gu[e]
        gate = jnp.minimum(gu[..., :D_FF], SWIGLU_LIMIT)
        up = jnp.clip(gu[..., D_FF:], -SWIGLU_LIMIT, SWIGLU_LIMIT)
        act = gate * jax.nn.sigmoid(SWIGLU_ALPHA * gate) * (up + 1.0)
        yb = act @ w_down[e] + b_down[e]
        return yb.astype(F32) * wt[:, None]

    yb = lax.map(run_block, (slot_tok.reshape(n_blocks, MOE_BLOCK),
                             slot_w.reshape(n_blocks, MOE_BLOCK), block_e))
    out = jnp.zeros((T, D), F32).at[slot_tok].add(yb.reshape(P, D))
    return out.astype(h.dtype).reshape(B, L, D)


def setup_inputs(seed: int = 0) -> dict:
    key = jax.random.key(seed)
    ks = iter(jax.random.split(key, 40))
    nrm = lambda shape, s: jax.random.normal(next(ks), shape, F32) * s
    D = D_MODEL
    decay_lo = -math.log(DECAY_TARGET) / SLOW_DECAY_PCT
    decay_hi = -math.log(DECAY_TARGET) / FAST_DECAY_PCT
    return {
        "x_prompt": nrm((BATCH, SEQ, D), 1.0),
        "x_sample": nrm((DEC_BATCH, DEC_SEQ, D), 1.0),
        "ln_in_g": 1.0 + nrm((D,), 0.02),
        "ln_in_b": nrm((D,), 0.02),
        "w_in": nrm((DEPTH, D, IN_WIDTH), D ** -0.5),
        "conv_w": nrm((DEPTH, SHORT_CONV, 3 * HY_WIDTH), SHORT_CONV ** -0.5),
        "conv_b": nrm((DEPTH, 3 * HY_WIDTH), 0.02),
        "filt_w1": nrm((DEPTH, FILTER_EMB, FILTER_HIDDEN), FILTER_EMB ** -0.5),
        "filt_b1": nrm((DEPTH, FILTER_HIDDEN), 0.1),
        "filt_freq1": 1.0 + nrm((DEPTH, FILTER_HIDDEN), 0.1),
        "filt_w2": nrm((DEPTH, FILTER_HIDDEN, FILTER_HIDDEN), FILTER_HIDDEN ** -0.5),
        "filt_b2": nrm((DEPTH, FILTER_HIDDEN), 0.1),
        "filt_freq2": 1.0 + nrm((DEPTH, FILTER_HIDDEN), 0.1),
        "filt_w3": nrm((DEPTH, FILTER_HIDDEN, 2 * HY_WIDTH), 0.05 * FILTER_HIDDEN ** -0.5),
        "filt_decay": jax.random.uniform(next(ks), (DEPTH, 2 * HY_WIDTH), F32, decay_lo, decay_hi),
        "hy_bias": nrm((DEPTH, HY_WIDTH), 1.0),
        "w_hy_out": nrm((DEPTH, HY_WIDTH, D), HY_WIDTH ** -0.5),
        "w_fn_out": nrm((DEPTH, FN_WIDTH, D), FN_WIDTH ** -0.5),
        "w_o": nrm((DEPTH, D, D), DEEPNORM_BETA * D ** -0.5),
        "ln1_g": 1.0 + nrm((DEPTH, D), 0.02),
        "ln1_b": nrm((DEPTH, D), 0.02),
        "w_router": nrm((DEPTH, D, N_EXPERTS), D ** -0.5),
        "b_router": nrm((DEPTH, N_EXPERTS), 0.01),
        "w_gu": nrm((DEPTH, N_EXPERTS, D, 2 * D_FF), D ** -0.5),
        "b_gu": nrm((DEPTH, N_EXPERTS, 2 * D_FF), 0.02),
        "w_down": nrm((DEPTH, N_EXPERTS, D_FF, D), DEEPNORM_BETA * D_FF ** -0.5),
        "b_down": nrm((DEPTH, N_EXPERTS, D), 0.02),
        "ln2_g": 1.0 + nrm((DEPTH, D), 0.02),
        "ln2_b": nrm((DEPTH, D), 0.02),
    }


def reference(x_prompt, x_sample, ln_in_g, ln_in_b, w_in, conv_w, conv_b, filt_w1, filt_b1,
              filt_freq1, filt_w2, filt_b2, filt_freq2, filt_w3, filt_decay, hy_bias, w_hy_out,
              w_fn_out, w_o, ln1_g, ln1_b, w_router, b_router, w_gu, b_gu, w_down, b_down,
              ln2_g, ln2_b):
    def trunk(x):
        h = layer_norm(x, ln_in_g, ln_in_b)
        for l in range(DEPTH):
            m = token_mixer(h, w_in[l], conv_w[l], conv_b[l], filt_w1[l], filt_b1[l],
                            filt_freq1[l], filt_w2[l], filt_b2[l], filt_freq2[l], filt_w3[l],
                            filt_decay[l], hy_bias[l], w_hy_out[l], w_fn_out[l], w_o[l])
            h = layer_norm(DEEPNORM_ALPHA * h + m, ln1_g[l], ln1_b[l])
            f = moe(h, w_router[l], b_router[l], w_gu[l], b_gu[l], w_down[l], b_down[l])
            h = layer_norm(DEEPNORM_ALPHA * h + f, ln2_g[l], ln2_b[l])
        return h

    y_prompt = trunk(x_prompt)
    y_sample = trunk(x_sample)
    return (y_prompt, y_sample)
```

```python
import functools
import math

import jax
import jax.numpy as jnp
from jax import lax
from jax.experimental import pallas as pl
from jax.experimental.pallas import tpu as pltpu

F32 = jnp.float32
BF16 = jnp.bfloat16
I32 = jnp.int32

TOP_K = 4
FN_GROUPS = 4
MOE_BLOCK = 512
FILTER_BANDS = 16
MOD_SHIFT = 0.05
SWIGLU_LIMIT = 7.0
SWIGLU_ALPHA = 1.702
LN_EPS = 1e-5
LANES = 128
DFT_RADIX = 128
FNET_MINOR = 64
VMEM_LIMIT = 48 * 1024 * 1024


def _cparams(ndim, vmem=VMEM_LIMIT):
    return pltpu.CompilerParams(dimension_semantics=("arbitrary",) * ndim,
                                vmem_limit_bytes=vmem)


def _layer_norm(x, g, b):
    mu = jnp.mean(x, axis=-1, keepdims=True)
    xc = x - mu
    var = jnp.mean(xc * xc, axis=-1, keepdims=True)
    return xc * lax.rsqrt(var + LN_EPS) * g + b


def _ln_in_kernel(n_first, xa_ref, xb_ref, g_ref, b_ref, h_ref, hb_ref):
    i = pl.program_id(0)

    def emit(x_ref):
        y = _layer_norm(x_ref[...], g_ref[...], b_ref[...])
        h_ref[...] = y
        hb_ref[...] = y.astype(BF16)

    @pl.when(i < n_first)
    def _():
        emit(xa_ref)

    @pl.when(i >= n_first)
    def _():
        emit(xb_ref)


def _ln_in(xa, xb, g, b, tm):
    ta, d = xa.shape
    tb = xb.shape[0]
    na, nb = ta // tm, tb // tm
    t = ta + tb
    return pl.pallas_call(
        functools.partial(_ln_in_kernel, na),
        grid=(na + nb,),
        in_specs=[
            pl.BlockSpec((tm, d), lambda i: (jnp.minimum(i, na - 1), 0)),
            pl.BlockSpec((tm, d), lambda i: (jnp.maximum(i - na, 0), 0)),
            pl.BlockSpec((1, d), lambda i: (0, 0)),
            pl.BlockSpec((1, d), lambda i: (0, 0)),
        ],
        out_specs=[pl.BlockSpec((tm, d), lambda i: (i, 0)),
                   pl.BlockSpec((tm, d), lambda i: (i, 0))],
        out_shape=[jax.ShapeDtypeStruct((t, d), F32), jax.ShapeDtypeStruct((t, d), BF16)],
        compiler_params=_cparams(1),
        name="ln_in",
    )(xa, xb, g.reshape(1, d), b.reshape(1, d))


def _mm_kernel(act, x_ref, w_ref, o_ref):
    acc = jnp.dot(x_ref[...], w_ref[...], preferred_element_type=F32)
    if act == "sigmoid":
        acc = jax.nn.sigmoid(acc)
    o_ref[...] = acc.astype(o_ref.dtype)


def _mm(x, w, act, tm, tn, out_dtype=BF16):
    t, k = x.shape
    n = w.shape[1]
    tn = min(tn, n)
    return pl.pallas_call(
        functools.partial(_mm_kernel, act),
        grid=(t // tm, n // tn),
        in_specs=[pl.BlockSpec((tm, k), lambda i, j: (i, 0)),
                  pl.BlockSpec((k, tn), lambda i, j: (0, j))],
        out_specs=pl.BlockSpec((tm, tn), lambda i, j: (i, j)),
        out_shape=jax.ShapeDtypeStruct((t, n), out_dtype),
        compiler_params=_cparams(2),
        name="proj_" + (act or "lin"),
    )(x, w)


HALO = 16


def _hyena_pre_kernel(tl, t_first, l_first, l_second, hy, z_ref, zp_ref, zn_ref, w_ref, b_ref,
                      x0_ref, vv_ref):
    i = pl.program_id(0)
    r0 = i * tl
    seq = jnp.where(r0 < t_first, l_first, l_second)
    is_first = lax.rem(r0, seq) == 0
    is_last = lax.rem(r0 + tl, seq) == 0
    u = z_ref[...].astype(F32)
    prev_row = jnp.where(is_first, 0.0, zp_ref[HALO - 1:HALO, :].astype(F32))
    next_row = jnp.where(is_last, 0.0, zn_ref[0:1, :].astype(F32))
    row = lax.broadcasted_iota(I32, u.shape, 0)
    um1 = jnp.where(row == 0, prev_row, pltpu.roll(u, 1, 0))
    up1 = jnp.where(row == tl - 1, next_row, pltpu.roll(u, tl - 1, 0))
    y = um1 * w_ref[0:1, :] + u * w_ref[1:2, :] + up1 * w_ref[2:3, :] + b_ref[...]
    x0_ref[...] = y[:, :hy].astype(BF16)
    vv_ref[...] = (y[:, 2 * hy:] * y[:, hy:2 * hy]).astype(BF16)


def _hyena_pre(z_hy, conv_w, conv_b, t_first, l_first, l_second, tl):
    t, c3 = z_hy.shape
    hy = c3 // 3
    nh = tl // HALO
    last = t // HALO - 1
    return pl.pallas_call(
        functools.partial(_hyena_pre_kernel, tl, t_first, l_first, l_second, hy),
        grid=(t // tl,),
        in_specs=[
            pl.BlockSpec((tl, c3), lambda i: (i, 0)),
            pl.BlockSpec((HALO, c3), lambda i: (jnp.maximum(i * nh - 1, 0), 0)),
            pl.BlockSpec((HALO, c3), lambda i: (jnp.minimum((i + 1) * nh, last), 0)),
            pl.BlockSpec((3, c3), lambda i: (0, 0)),
            pl.BlockSpec((1, c3), lambda i: (0, 0)),
        ],
        out_specs=[pl.BlockSpec((tl, hy), lambda i: (i, 0)),
                   pl.BlockSpec((tl, hy), lambda i: (i, 0))],
        out_shape=[jax.ShapeDtypeStruct((t, hy), BF16), jax.ShapeDtypeStruct((t, hy), BF16)],
        compiler_params=_cparams(1),
        name="hyena_pre",
    )(z_hy, z_hy, z_hy, conv_w, conv_b.reshape(1, c3))


def _filter_kernel(tr, seq, w1_ref, b1_ref, f1_ref, w2_ref, b2_ref, f2_ref, w3_ref, dec_ref, o_ref):
    i = pl.program_id(0)
    n = i * tr + lax.broadcasted_iota(I32, (tr, 1), 0)
    m = jnp.where(n < seq, n, 2 * seq - n).astype(F32)
    t = m * (1.0 / (seq - 1))
    ang = (2.0 * math.pi / seq) * m
    band = lax.broadcasted_iota(I32, (1, FILTER_BANDS), 1).astype(F32)
    freqs = 1e-4 + band * ((FILTER_BANDS - 1 - 1e-4) / (FILTER_BANDS - 1))
    fa = ang * freqs
    hi = lax.Precision.HIGHEST
    pre = (t * w1_ref[0:1, :]
           + jnp.dot(jnp.cos(fa), w1_ref[1:1 + FILTER_BANDS, :], precision=hi, preferred_element_type=F32)
           - jnp.dot(jnp.sin(fa), w1_ref[1 + FILTER_BANDS:, :], precision=hi, preferred_element_type=F32))
    h = jnp.sin(f1_ref[...] * (pre + b1_ref[...]))
    h = jnp.sin(f2_ref[...] * (jnp.dot(h, w2_ref[...], precision=hi, preferred_element_type=F32)
                               + b2_ref[...]))
    h = jnp.dot(h, w3_ref[...], precision=hi, preferred_element_type=F32)
    h = h * (jnp.exp(-t * jnp.abs(dec_ref[...])) + MOD_SHIFT)
    o_ref[...] = jnp.where(n == seq, 0.0, h)


def _filter(seq, w1, b1, f1, w2, b2, f2, w3, dec, tr):
    hid = w1.shape[1]
    hy = w3.shape[1] // 2
    nfwd = seq // tr
    side = lambda i: (0, jnp.where(i >= nfwd, 1, 0))
    full = lambda i: (0, 0)
    return pl.pallas_call(
        functools.partial(_filter_kernel, tr, seq),
        grid=(2 * seq // tr,),
        in_specs=[
            pl.BlockSpec(w1.shape, full), pl.BlockSpec((1, hid), full), pl.BlockSpec((1, hid), full),
            pl.BlockSpec(w2.shape, full), pl.BlockSpec((1, hid), full), pl.BlockSpec((1, hid), full),
            pl.BlockSpec((hid, hy), side), pl.BlockSpec((1, hy), side),
        ],
        out_specs=pl.BlockSpec((tr, hy), lambda i: (i, 0)),
        out_shape=jax.ShapeDtypeStruct((2 * seq, hy), F32),
        compiler_params=_cparams(1),
        name="hyena_filter",
    )(w1, b1.reshape(1, hid), f1.reshape(1, hid), w2, b2.reshape(1, hid), f2.reshape(1, hid),
      w3, dec.reshape(1, 2 * hy))


def _blm_kernel(tj, shared, m_ref, x_ref, o_ref):
    for jj in range(tj):
        m = m_ref[0 if shared else jj]
        o_ref[0, jj] = jnp.dot(m, x_ref[0, jj], preferred_element_type=F32).astype(o_ref.dtype)


def _blm(m, x, tj, out_dtype=BF16):
    g, j, k, c = x.shape
    jm, mr, _ = m.shape
    shared = jm == 1
    tj = min(tj, j)
    m_spec = (pl.BlockSpec((1, mr, k), lambda jb, gb: (0, 0, 0)) if shared
              else pl.BlockSpec((tj, mr, k), lambda jb, gb: (jb, 0, 0)))
    return pl.pallas_call(
        functools.partial(_blm_kernel, tj, shared),
        grid=(j // tj, g),
        in_specs=[m_spec, pl.BlockSpec((1, tj, k, c), lambda jb, gb: (gb, jb, 0, 0))],
        out_specs=pl.BlockSpec((1, tj, mr, c), lambda jb, gb: (gb, jb, 0, 0)),
        out_shape=jax.ShapeDtypeStruct((g, j, mr, c), out_dtype),
        compiler_params=_cparams(2),
        name="dft_stage",
    )(m, x)


def _fnet_s1_kernel(tj, fn, m_ref, r_ref, x_ref, o_ref):
    for jj in range(tj):
        z = jnp.dot(x_ref[0, jj], r_ref[...], preferred_element_type=F32)
        zs = jnp.concatenate([z[:, :fn], z[:, fn:]], axis=0).astype(BF16)
        o_ref[0, jj] = jnp.dot(m_ref[jj], zs, preferred_element_type=F32).astype(o_ref.dtype)


def _fnet_s1(m, r, x, tj):
    g, j, k, fn = x.shape
    _, mr, k2 = m.shape
    tj = min(tj, j)
    return pl.pallas_call(
        functools.partial(_fnet_s1_kernel, tj, fn),
        grid=(j // tj, g),
        in_specs=[pl.BlockSpec((tj, mr, k2), lambda jb, gb: (jb, 0, 0)),
                  pl.BlockSpec(r.shape, lambda jb, gb: (0, 0)),
                  pl.BlockSpec((1, tj, k, fn), lambda jb, gb: (gb, jb, 0, 0))],
        out_specs=pl.BlockSpec((1, tj, mr, fn), lambda jb, gb: (gb, jb, 0, 0)),
        out_shape=jax.ShapeDtypeStruct((g, j, mr, fn), BF16),
        compiler_params=_cparams(2),
        name="fnet_stage1",
    )(m, r, x)


def _hyena_mid_kernel(tk, n2, f_ref, kf_ref, gi_ref, x_ref, o_ref):
    for kk in range(tk):
        a = jnp.dot(f_ref[...], x_ref[0, kk], preferred_element_type=F32)
        ar, ai = a[:n2], a[n2:]
        kr = kf_ref[kk, :n2].astype(F32)
        ki = kf_ref[kk, n2:].astype(F32)
        prod = jnp.concatenate([ar * kr - ai * ki, ar * ki + ai * kr], axis=0).astype(BF16)
        o_ref[0, kk] = jnp.dot(gi_ref[kk], prod, preferred_element_type=F32).astype(o_ref.dtype)


def _hyena_mid(f2s, kf, ginv, x, tk):
    g, n1, r, c = x.shape
    n2 = r // 2
    tk = min(tk, n1)
    return pl.pallas_call(
        functools.partial(_hyena_mid_kernel, tk, n2),
        grid=(n1 // tk, g),
        in_specs=[pl.BlockSpec((r, r), lambda kb, gb: (0, 0)),
                  pl.BlockSpec((tk, r, c), lambda kb, gb: (kb, 0, 0)),
                  pl.BlockSpec((tk, r, r), lambda kb, gb: (kb, 0, 0)),
                  pl.BlockSpec((1, tk, r, c), lambda kb, gb: (gb, kb, 0, 0))],
        out_specs=pl.BlockSpec((1, tk, r, c), lambda kb, gb: (gb, kb, 0, 0)),
        out_shape=jax.ShapeDtypeStruct(x.shape, BF16),
        compiler_params=_cparams(2),
        name="hyena_mid",
    )(f2s, kf, ginv, x)


def _cis(num, den, sign):
    ang = (2.0 * math.pi / den) * lax.rem(num, den).astype(F32)
    return jnp.cos(ang), sign * jnp.sin(ang)


def _stack_complex(mr, mi):
    top = jnp.concatenate([mr, -mi], axis=-1)
    bot = jnp.concatenate([mi, mr], axis=-1)
    return jnp.concatenate([top, bot], axis=-2)


def _iota(shape, axis):
    return lax.broadcasted_iota(I32, shape, axis)


def _hyena_mats(n, n1, n2):
    n1h = n1 // 2
    sh = (n2, n1, n1)
    gr, gi = _cis(_iota(sh, 1) * (n2 * _iota(sh, 2) + _iota(sh, 0)), n, -1.0)
    g_data = _stack_complex(gr[:, :, :n1h], gi[:, :, :n1h]).astype(BF16)
    g_filt = jnp.concatenate([gr, gi], axis=1).astype(BF16)
    sh = (n2, n2)
    fr, fi = _cis(_iota(sh, 0) * _iota(sh, 1), n2, -1.0)
    f2s = _stack_complex(fr, fi).astype(BF16)
    sh = (n1, n2, n2)
    ir, ii = _cis(_iota(sh, 1) * (_iota(sh, 0) + n1 * _iota(sh, 2)), n, 1.0)
    ginv = _stack_complex(ir, ii).astype(BF16)
    sh = (n1h, n1)
    br, bi = _cis(_iota(sh, 0) * _iota(sh, 1), n1, 1.0)
    f1inv = (_stack_complex(br, bi) * (1.0 / n)).astype(BF16)[None]
    return g_data, g_filt, f2s, ginv, f1inv


def _fnet_mats(seq, na, nb, fn):
    gd = fn // FN_GROUPS
    sh = (nb, na, na)
    gr, gi = _cis(_iota(sh, 1) * (nb * _iota(sh, 2) + _iota(sh, 0)), seq, -1.0)
    g1 = _stack_complex(gr, gi).astype(BF16)
    sh = (nb, nb)
    fr, fi = _cis(_iota(sh, 0) * _iota(sh, 1), nb, -1.0)
    scale = 1.0 / math.sqrt(seq * gd)
    f2re = (jnp.concatenate([fr, -fi], axis=-1) * scale).astype(BF16)[None]
    sh = (fn, fn)
    same = (_iota(sh, 0) // gd) == (_iota(sh, 1) // gd)
    cr, ci = _cis(_iota(sh, 0) * _iota(sh, 1), gd, -1.0)
    rmat = jnp.concatenate([jnp.where(same, cr, 0.0), jnp.where(same, ci, 0.0)], axis=1).astype(BF16)
    return g1, f2re, rmat


def _hyena_conv(vv, batch, seq, filt_args, tj):
    c = vv.shape[1]
    n = 2 * seq
    n1 = DFT_RADIX
    n2 = n // n1
    n1h = n1 // 2
    bp = batch // 2
    g_data, g_filt, f2s, ginv, f1inv = _hyena_mats(n, n1, n2)
    kt = _filter(seq, *filt_args, tr=min(512, seq))
    kt = kt.reshape(1, n1, n2, c).transpose(0, 2, 1, 3).astype(BF16)
    ks = _blm(g_filt, kt, tj)
    ks = ks.reshape(1, n2, 2, n1, c).transpose(0, 3, 2, 1, 4).reshape(1, n1, 2 * n2, c)
    kf = _blm(f2s[None], ks, tj, out_dtype=F32)[0]
    x = vv.reshape(bp, 2, n1h, n2, c).transpose(0, 3, 1, 2, 4).reshape(bp, n2, n1, c)
    a = _blm(g_data, x, tj)
    a = a.reshape(bp, n2, 2, n1, c).transpose(0, 3, 2, 1, 4).reshape(bp, n1, 2 * n2, c)
    z = _hyena_mid(f2s, kf, ginv, a, 4)
    z = z.reshape(bp, n1, 2, n2, c).transpose(0, 3, 2, 1, 4).reshape(bp, n2, 2 * n1, c)
    y = _blm(f1inv, z, tj)
    y = y.reshape(bp, n2, 2, n1h, c).transpose(0, 2, 3, 1, 4)
    return y.reshape(batch * seq, c)


def _fnet_mix(u, batch, seq, tj):
    fn = u.shape[1]
    nb = FNET_MINOR
    na = seq // nb
    g1, f2re, rmat = _fnet_mats(seq, na, nb, fn)
    x = u.reshape(batch, na, nb, fn).transpose(0, 2, 1, 3)
    a = _fnet_s1(g1, rmat, x, tj)
    a = a.reshape(batch, nb, 2, na, fn).transpose(0, 3, 2, 1, 4).reshape(batch, na, 2 * nb, fn)
    y = _blm(f2re, a, tj)
    return y.transpose(0, 2, 1, 3).reshape(batch * seq, fn)


def _merge_kernel(alpha, d, n_exp, conv_ref, vv_ref, x0_ref, yfn_ref, gate_ref, h_ref, bias_ref,
                  why_ref, wfn_ref, wo_ref, g_ref, b_ref, wr_ref, br_ref,
                  h1_ref, idx_ref, wt_ref):
    vv = vv_ref[...].astype(F32)
    y_hy = x0_ref[...].astype(F32) * (conv_ref[...].astype(F32) + vv * bias_ref[...])
    a = jnp.dot(y_hy.astype(BF16), why_ref[...], preferred_element_type=F32)
    bfn = jnp.dot(yfn_ref[...], wfn_ref[...], preferred_element_type=F32)
    gates = gate_ref[...].astype(F32)
    merged = gates[:, :d] * a + gates[:, d:] * bfn
    m = jnp.dot(merged.astype(BF16), wo_ref[...], preferred_element_type=F32)
    h1 = _layer_norm(alpha * h_ref[...] + m, g_ref[...], b_ref[...])
    h1_ref[...] = h1
    logits = jnp.dot(h1, wr_ref[...], precision=lax.Precision.HIGHEST,
                     preferred_element_type=F32) + br_ref[...]
    lane = lax.broadcasted_iota(I32, logits.shape, 1)
    lane_f = lane.astype(F32)
    neg = jnp.float32(-jnp.inf)
    logits = jnp.where(lane < n_exp, logits, neg)
    vals, idxs = [], []
    for _ in range(TOP_K):
        mx = jnp.max(logits, axis=-1, keepdims=True)
        ix = jnp.min(jnp.where(logits == mx, lane_f, float(LANES)), axis=-1, keepdims=True).astype(I32)
        vals.append(mx)
        idxs.append(ix)
        logits = jnp.where(lane == ix, neg, logits)
    exps = [jnp.exp(v - vals[0]) for v in vals]
    den = exps[0]
    for e in exps[1:]:
        den = den + e
    idx_out = jnp.zeros(lane.shape, I32)
    wt_out = jnp.zeros(lane.shape, F32)
    for k in range(TOP_K):
        idx_out = jnp.where(lane == k, idxs[k], idx_out)
        wt_out = jnp.where(lane == k, exps[k] / den, wt_out)
    idx_ref[...] = idx_out
    wt_ref[...] = wt_out


def _merge(alpha, conv, vv, x0, yfn, gates, h, bias, why, wfn, wo, g, b, wr, br, tm):
    t, d = h.shape
    hy = conv.shape[1]
    fn = yfn.shape[1]
    n_exp = wr.shape[1]
    wr_p = jnp.zeros((d, LANES), F32).at[:, :n_exp].set(wr)
    br_p = jnp.zeros((1, LANES), F32).at[0, :n_exp].set(br)
    row = lambda c: pl.BlockSpec((tm, c), lambda i: (i, 0))
    full = lambda a: pl.BlockSpec(a.shape, lambda i: (0,) * a.ndim)
    args = (conv, vv, x0, yfn, gates, h, bias.reshape(1, hy), why, wfn, wo,
            g.reshape(1, d), b.reshape(1, d), wr_p, br_p)
    in_specs = [row(hy), row(hy), row(hy), row(fn), row(2 * d), row(d)] + [full(a) for a in args[6:]]
    return pl.pallas_call(
        functools.partial(_merge_kernel, alpha, d, n_exp),
        grid=(t // tm,),
        in_specs=in_specs,
        out_specs=[row(d), row(LANES), row(LANES)],
        out_shape=[jax.ShapeDtypeStruct((t, d), F32), jax.ShapeDtypeStruct((t, LANES), I32),
                   jax.ShapeDtypeStruct((t, LANES), F32)],
        compiler_params=_cparams(1),
        name="merge_route",
    )(*args)


def _rank_kernel(tm, idx_ref, rank_ref, cnt_ref, carry_ref):
    i = pl.program_id(0)

    @pl.when(i == 0)
    def _():
        carry_ref[...] = jnp.zeros_like(carry_ref)

    idx = idx_ref[...]
    lane = lax.broadcasted_iota(I32, idx.shape, 1)
    cols = [idx[:, k:k + 1] for k in range(TOP_K)]
    onehot = jnp.zeros(idx.shape, F32)
    for ck in cols:
        onehot = onehot + (lane == ck).astype(F32)
    r = lax.broadcasted_iota(I32, (tm, tm), 0)
    c = lax.broadcasted_iota(I32, (tm, tm), 1)
    tri = (c < r).astype(BF16)
    before = jnp.dot(tri, onehot.astype(BF16), preferred_element_type=F32) + carry_ref[...]
    out = jnp.zeros(idx.shape, F32)
    for k, ck in enumerate(cols):
        rk = jnp.sum(jnp.where(lane == ck, before, 0.0), axis=-1, keepdims=True)
        out = jnp.where(lane == k, rk, out)
    rank_ref[...] = out.astype(I32)
    total = carry_ref[...] + jnp.sum(onehot, axis=0, keepdims=True)
    carry_ref[...] = total
    cnt_ref[...] = total.astype(I32)


def _ranks(idx128, tm):
    t = idx128.shape[0]
    return pl.pallas_call(
        functools.partial(_rank_kernel, tm),
        grid=(t // tm,),
        in_specs=[pl.BlockSpec((tm, LANES), lambda i: (i, 0))],
        out_specs=[pl.BlockSpec((tm, LANES), lambda i: (i, 0)),
                   pl.BlockSpec((1, LANES), lambda i: (0, 0))],
        out_shape=[jax.ShapeDtypeStruct((t, LANES), I32), jax.ShapeDtypeStruct((1, LANES), I32)],
        scratch_shapes=[pltpu.VMEM((1, LANES), F32)],
        compiler_params=_cparams(1),
        name="expert_rank",
    )(idx128)


def _row_copy(src_ref, src_row, dst_ref, dst_row, sem):
    return pltpu.make_async_copy(src_ref.at[pl.ds(src_row, 1)], dst_ref.at[pl.ds(dst_row, 1)], sem)


def _dispatch_kernel(tm, n_exp, n_blocks, zstart_ref, znum_ref, nused_ref, dest_hbm, h_ref, xs_hbm,
                     dsm, zbuf, sem_idx, sem, sem_z):
    i = pl.program_id(0)
    fetch = pltpu.make_async_copy(dest_hbm.at[i], dsm, sem_idx)
    fetch.start()
    fetch.wait()

    def issue(r, carry):
        for k in range(TOP_K):
            _row_copy(h_ref, r, xs_hbm, dsm[r * TOP_K + k], sem).start()
        return carry

    lax.fori_loop(0, tm, issue, 0)
    for _ in range(TOP_K):
        pltpu.make_async_copy(h_ref, xs_hbm.at[pl.ds(0, tm)], sem).wait()

    @pl.when(i == pl.num_programs(0) - 1)
    def _():
        zbuf[...] = jnp.zeros_like(zbuf)

        def per_expert(e, carry):
            start = zstart_ref[e]
            num = znum_ref[e]

            def zissue(r, c2):
                _row_copy(zbuf, 0, xs_hbm, start + r, sem_z).start()
                return c2

            def zwait(r, c2):
                _row_copy(zbuf, 0, xs_hbm, 0, sem_z).wait()
                return c2

            lax.fori_loop(0, num, zissue, 0)
            lax.fori_loop(0, num, zwait, 0)
            return carry

        lax.fori_loop(0, n_exp, per_expert, 0)

        def blk_copy(bi):
            row = pl.multiple_of(bi * MOE_BLOCK, MOE_BLOCK)
            return pltpu.make_async_copy(zbuf, xs_hbm.at[pl.ds(row, MOE_BLOCK)], sem_z)

        def bissue(bi, carry):
            blk_copy(bi).start()
            return carry

        def bwait(bi, carry):
            blk_copy(bi).wait()
            return carry

        lax.fori_loop(nused_ref[0], n_blocks, bissue, 0)
        lax.fori_loop(nused_ref[0], n_blocks, bwait, 0)


def _dispatch(zstart, znum, n_used, dest, h1, n_slots, tm):
    t, d = h1.shape
    n_exp = zstart.shape[0]
    dest2 = dest.reshape(t // tm, tm * TOP_K)
    return pl.pallas_call(
        functools.partial(_dispatch_kernel, tm, n_exp, n_slots // MOE_BLOCK),
        grid_spec=pltpu.PrefetchScalarGridSpec(
            num_scalar_prefetch=3,
            grid=(t // tm,),
            in_specs=[pl.BlockSpec(memory_space=pl.ANY),
                      pl.BlockSpec((tm, d), lambda i, zs, zn, nu: (i, 0))],
            out_specs=pl.BlockSpec(memory_space=pl.ANY),
            scratch_shapes=[pltpu.SMEM((tm * TOP_K,), I32), pltpu.VMEM((MOE_BLOCK, d), F32),
                            pltpu.SemaphoreType.DMA, pltpu.SemaphoreType.DMA,
                            pltpu.SemaphoreType.DMA],
        ),
        out_shape=jax.ShapeDtypeStruct((n_slots, d), F32),
        compiler_params=_cparams(1),
        name="moe_dispatch",
    )(zstart, znum, n_used, dest2, h1)


def _expert_kernel(d_ff, be_ref, nused_ref, x_ref, wgu_ref, bgu_ref, wd_ref, bd_ref, y_ref):
    i = pl.program_id(0)

    @pl.when(i < nused_ref[0])
    def _():
        x = x_ref[...].astype(BF16)
        gu = jnp.dot(x, wgu_ref[0], preferred_element_type=F32) + bgu_ref[0]
        gate = jnp.minimum(gu[:, :d_ff], SWIGLU_LIMIT)
        up = jnp.clip(gu[:, d_ff:], -SWIGLU_LIMIT, SWIGLU_LIMIT)
        act = gate * jax.nn.sigmoid(SWIGLU_ALPHA * gate) * (up + 1.0)
        y_ref[...] = jnp.dot(act.astype(BF16), wd_ref[0], preferred_element_type=F32) + bd_ref[0]

    @pl.when(i >= nused_ref[0])
    def _():
        y_ref[...] = jnp.zeros_like(y_ref)


def _experts(block_e, n_used, xs, wgu, bgu, wd, bd):
    n_slots, d = xs.shape
    n_exp, _, two_ff = wgu.shape
    d_ff = two_ff // 2
    nb = n_slots // MOE_BLOCK
    live = lambda i, be, nu: jnp.minimum(i, nu[0] - 1)
    return pl.pallas_call(
        functools.partial(_expert_kernel, d_ff),
        grid_spec=pltpu.PrefetchScalarGridSpec(
            num_scalar_prefetch=2,
            grid=(nb,),
            in_specs=[
                pl.BlockSpec((MOE_BLOCK, d), lambda i, be, nu: (live(i, be, nu), 0)),
                pl.BlockSpec((1, d, two_ff), lambda i, be, nu: (be[live(i, be, nu)], 0, 0)),
                pl.BlockSpec((1, 1, two_ff), lambda i, be, nu: (be[live(i, be, nu)], 0, 0)),
                pl.BlockSpec((1, d_ff, d), lambda i, be, nu: (be[live(i, be, nu)], 0, 0)),
                pl.BlockSpec((1, 1, d), lambda i, be, nu: (be[live(i, be, nu)], 0, 0)),
            ],
            out_specs=pl.BlockSpec((MOE_BLOCK, d), lambda i, be, nu: (i, 0)),
        ),
        out_shape=jax.ShapeDtypeStruct((n_slots, d), F32),
        compiler_params=_cparams(1),
        name="moe_experts",
    )(block_e, n_used, xs, wgu, bgu.reshape(n_exp, 1, two_ff), wd, bd.reshape(n_exp, 1, d))


def _combine_kernel(tm, alpha, n_first, split, dest_hbm, h1_ref, wt_ref, g_ref, b_ref, y_hbm, *rest):
    if split:
        oa_ref, ob_ref, dsm, ybuf, sem_idx, sem = rest
    else:
        o_ref, ob16_ref, dsm, ybuf, sem_idx, sem = rest
    i = pl.program_id(0)
    fetch = pltpu.make_async_copy(dest_hbm.at[i], dsm, sem_idx)
    fetch.start()
    fetch.wait()

    def issue(r, carry):
        for k in range(TOP_K):
            _row_copy(y_hbm, dsm[r * TOP_K + k], ybuf.at[k], r, sem).start()
        return carry

    lax.fori_loop(0, tm, issue, 0)
    for k in range(TOP_K):
        pltpu.make_async_copy(y_hbm.at[pl.ds(0, tm)], ybuf.at[k], sem).wait()
    wt = wt_ref[...]
    f = wt[:, 0:1] * ybuf[0]
    for k in range(1, TOP_K):
        f = f + wt[:, k:k + 1] * ybuf[k]
    out = _layer_norm(alpha * h1_ref[...] + f, g_ref[...], b_ref[...])
    if split:
        @pl.when(i < n_first)
        def _():
            oa_ref[...] = out

        @pl.when(i >= n_first)
        def _():
            ob_ref[...] = out
    else:
        o_ref[...] = out
        ob16_ref[...] = out.astype(BF16)


def _combine(alpha, dest, h1, wt128, g, b, y, tm, split_rows=None):
    t, d = h1.shape
    nsteps = t // tm
    dest2 = dest.reshape(nsteps, tm * TOP_K)
    split = split_rows is not None
    row = lambda c: pl.BlockSpec((tm, c), lambda i: (i, 0))
    vec = pl.BlockSpec((1, d), lambda i: (0, 0))
    if split:
        na = split_rows // tm
        out_specs = [pl.BlockSpec((tm, d), lambda i: (jnp.minimum(i, na - 1), 0)),
                     pl.BlockSpec((tm, d), lambda i: (jnp.maximum(i - na, 0), 0))]
        out_shape = [jax.ShapeDtypeStruct((split_rows, d), F32),
                     jax.ShapeDtypeStruct((t - split_rows, d), F32)]
    else:
        na = 0
        out_specs = [row(d), row(d)]
        out_shape = [jax.ShapeDtypeStruct((t, d), F32), jax.ShapeDtypeStruct((t, d), BF16)]
    return pl.pallas_call(
        functools.partial(_combine_kernel, tm, alpha, na, split),
        grid=(nsteps,),
        in_specs=[pl.BlockSpec(memory_space=pl.ANY), row(d), row(LANES), vec, vec,
                  pl.BlockSpec(memory_space=pl.ANY)],
        out_specs=out_specs,
        out_shape=out_shape,
        scratch_shapes=[pltpu.SMEM((tm * TOP_K,), I32), pltpu.VMEM((TOP_K, tm, d), F32),
                        pltpu.SemaphoreType.DMA, pltpu.SemaphoreType.DMA],
        compiler_params=_cparams(1),
        name="moe_combine",
    )(dest2, h1, wt128, g.reshape(1, d), b.reshape(1, d), y)


def _route_plan(idx, rank, counts, n_blocks):
    padded = (counts + MOE_BLOCK - 1) // MOE_BLOCK * MOE_BLOCK
    pad_ends = jnp.cumsum(padded)
    pad_starts = pad_ends - padded
    dest = pad_starts[idx] + rank
    block_start = jnp.arange(n_blocks, dtype=I32) * MOE_BLOCK
    n_exp = counts.shape[0]
    block_e = jnp.minimum(jnp.searchsorted(pad_ends, block_start, side="right"), n_exp - 1).astype(I32)
    n_used = (pad_ends[-1] // MOE_BLOCK).astype(I32).reshape(1)
    return dest.astype(I32), block_e, n_used, (pad_starts + counts).astype(I32), (padded - counts).astype(I32)


def _tile(n, pref):
    t = min(pref, n)
    while n % t:
        t //= 2
    return t


def kernel(x_prompt, x_sample, ln_in_g, ln_in_b, w_in, conv_w, conv_b, filt_w1, filt_b1, filt_freq1, filt_w2, filt_b2, filt_freq2, filt_w3, filt_decay, hy_bias, w_hy_out, w_fn_out, w_o, ln1_g, ln1_b, w_router, b_router, w_gu, b_gu, w_down, b_down, ln2_g, ln2_b):
    bp, lp, d = x_prompt.shape
    bs, ls, _ = x_sample.shape
    depth = w_in.shape[0]
    hy = hy_bias.shape[-1]
    fn = w_fn_out.shape[1]
    n_exp = w_router.shape[-1]
    tp, ts = bp * lp, bs * ls
    t = tp + ts
    alpha = (2 * depth) ** 0.25
    trunks = ((0, tp, bp, lp), (tp, ts, bs, ls))
    n_blocks = -(-(t * TOP_K) // MOE_BLOCK) + n_exp
    n_slots = n_blocks * MOE_BLOCK

    tm_ln = _tile(math.gcd(tp, ts), 1024)
    tm_mm = _tile(t, 1024)
    tl = _tile(math.gcd(lp, ls), 256)
    tm_tok = _tile(t, 512)
    tm_row = _tile(math.gcd(tp, ts), 256)

    h, hb = _ln_in(x_prompt.reshape(tp, d), x_sample.reshape(ts, d), ln_in_g, ln_in_b, tm_ln)
    out = None
    for l in range(depth):
        w_in_l = w_in[l].astype(BF16)
        z_hy = _mm(hb, w_in_l[:, :3 * hy], None, tm_mm, 768)
        z_fn = _mm(hb, w_in_l[:, 3 * hy:3 * hy + fn], None, tm_mm, 256)
        gates = _mm(hb, w_in_l[:, 3 * hy + fn:], "sigmoid", tm_mm, 1024)
        x0, vv = _hyena_pre(z_hy, conv_w[l], conv_b[l], tp, lp, ls, tl)
        filt_args = (filt_w1[l], filt_b1[l], filt_freq1[l], filt_w2[l], filt_b2[l], filt_freq2[l],
                     filt_w3[l], filt_decay[l])
        conv = jnp.concatenate(
            [_hyena_conv(vv[o:o + n], b, s, filt_args, 8) for (o, n, b, s) in trunks], axis=0)
        yfn = jnp.concatenate(
            [_fnet_mix(z_fn[o:o + n], b, s, 8) for (o, n, b, s) in trunks], axis=0)
        h1, idx128, wt128 = _merge(alpha, conv, vv, x0, yfn, gates, h, hy_bias[l],
                                   w_hy_out[l].astype(BF16), w_fn_out[l].astype(BF16),
                                   w_o[l].astype(BF16), ln1_g[l], ln1_b[l], w_router[l], b_router[l],
                                   tm_tok)
        rank128, cnt128 = _ranks(idx128, tm_tok)
        dest, block_e, n_used, zstart, znum = _route_plan(
            idx128[:, :TOP_K], rank128[:, :TOP_K], cnt128[0, :n_exp], n_blocks)
        xs = _dispatch(zstart, znum, n_used, dest, h1, n_slots, tm_row)
        y = _experts(block_e, n_used, xs, w_gu[l].astype(BF16), b_gu[l], w_down[l].astype(BF16),
                     b_down[l])
        if l + 1 < depth:
            h, hb = _combine(alpha, dest, h1, wt128, ln2_g[l], ln2_b[l], y, tm_row)
        else:
            out = _combine(alpha, dest, h1, wt128, ln2_g[l], ln2_b[l], y, tm_row, split_rows=tp)
    return (out[0].reshape(bp, lp, d), out[1].reshape(bs, ls, d))
```

```python
import functools
import math

import jax
import jax.numpy as jnp
from jax import lax
from jax.experimental import pallas as pl
from jax.experimental.pallas import tpu as pltpu

F32 = jnp.float32
BF16 = jnp.bfloat16
I32 = jnp.int32

TOP_K = 4
FN_GROUPS = 4
MOE_BLOCK = 512
FILTER_BANDS = 16
MOD_SHIFT = 0.05
SWIGLU_LIMIT = 7.0
SWIGLU_ALPHA = 1.702
LN_EPS = 1e-5
LANES = 128
DFT_RADIX = 128
FNET_MINOR = 64
VMEM_LIMIT = 48 * 1024 * 1024


def _cparams(ndim, vmem=VMEM_LIMIT):
    return pltpu.CompilerParams(dimension_semantics=("arbitrary",) * ndim,
                                vmem_limit_bytes=vmem)


def _layer_norm(x, g, b):
    mu = jnp.mean(x, axis=-1, keepdims=True)
    xc = x - mu
    var = jnp.mean(xc * xc, axis=-1, keepdims=True)
    return xc * lax.rsqrt(var + LN_EPS) * g + b


def _ln_in_kernel(n_first, xa_ref, xb_ref, g_ref, b_ref, h_ref, hb_ref):
    i = pl.program_id(0)

    def emit(x_ref):
        y = _layer_norm(x_ref[...], g_ref[...], b_ref[...])
        h_ref[...] = y
        hb_ref[...] = y.astype(BF16)

    @pl.when(i < n_first)
    def _():
        emit(xa_ref)

    @pl.when(i >= n_first)
    def _():
        emit(xb_ref)


def _ln_in(xa, xb, g, b, tm):
    ta, d = xa.shape
    tb = xb.shape[0]
    na, nb = ta // tm, tb // tm
    t = ta + tb
    return pl.pallas_call(
        functools.partial(_ln_in_kernel, na),
        grid=(na + nb,),
        in_specs=[
            pl.BlockSpec((tm, d), lambda i: (jnp.minimum(i, na - 1), 0)),
            pl.BlockSpec((tm, d), lambda i: (jnp.maximum(i - na, 0), 0)),
            pl.BlockSpec((1, d), lambda i: (0, 0)),
            pl.BlockSpec((1, d), lambda i: (0, 0)),
        ],
        out_specs=[pl.BlockSpec((tm, d), lambda i: (i, 0)),
                   pl.BlockSpec((tm, d), lambda i: (i, 0))],
        out_shape=[jax.ShapeDtypeStruct((t, d), F32), jax.ShapeDtypeStruct((t, d), BF16)],
        compiler_params=_cparams(1),
        name="ln_in",
    )(xa, xb, g.reshape(1, d), b.reshape(1, d))


HALO = 16


def _proj_kernel(tm, t_first, l_first, l_second, hy, fn, x_ref, xp_ref, xn_ref, w_ref, cw_ref, cb_ref,
                 x0_ref, vv_ref, zfn_ref, gate_ref):
    i = pl.program_id(0)
    r0 = i * tm
    seq = jnp.where(r0 < t_first, l_first, l_second)
    is_first = lax.rem(r0, seq) == 0
    is_last = lax.rem(r0 + tm, seq) == 0
    x = x_ref[...]
    x_ext = jnp.concatenate([xp_ref[...], x, xn_ref[...]], axis=0)
    row = lax.broadcasted_iota(I32, (tm, 1), 0)
    kill_prev = jnp.logical_and(row == 0, is_first)
    kill_next = jnp.logical_and(row == tm - 1, is_last)

    def conv_chunk(c):
        lo, hi = c * hy, (c + 1) * hy
        z = jnp.dot(x_ext, w_ref[:, lo:hi], preferred_element_type=F32)
        um1 = jnp.where(kill_prev, 0.0, z[HALO - 1:HALO - 1 + tm])
        up1 = jnp.where(kill_next, 0.0, z[HALO + 1:HALO + 1 + tm])
        return (um1 * cw_ref[0:1, lo:hi] + z[HALO:HALO + tm] * cw_ref[1:2, lo:hi]
                + up1 * cw_ref[2:3, lo:hi] + cb_ref[:, lo:hi])

    x0_ref[...] = conv_chunk(0).astype(BF16)
    vv_ref[...] = (conv_chunk(2) * conv_chunk(1)).astype(BF16)
    zfn_ref[...] = jnp.dot(x, w_ref[:, 3 * hy:3 * hy + fn], preferred_element_type=F32).astype(BF16)
    g = jnp.dot(x, w_ref[:, 3 * hy + fn:], preferred_element_type=F32)
    gate_ref[...] = jax.nn.sigmoid(g).astype(BF16)


def _proj(hb, w_in, conv_w, conv_b, hy, fn, t_first, l_first, l_second, tm):
    t, d = hb.shape
    n = w_in.shape[1]
    ng = n - 3 * hy - fn
    nh = tm // HALO
    last = t // HALO - 1
    row = lambda c: pl.BlockSpec((tm, c), lambda i: (i, 0))
    return pl.pallas_call(
        functools.partial(_proj_kernel, tm, t_first, l_first, l_second, hy, fn),
        grid=(t // tm,),
        in_specs=[
            row(d),
            pl.BlockSpec((HALO, d), lambda i: (jnp.maximum(i * nh - 1, 0), 0)),
            pl.BlockSpec((HALO, d), lambda i: (jnp.minimum((i + 1) * nh, last), 0)),
            pl.BlockSpec((d, n), lambda i: (0, 0)),
            pl.BlockSpec((3, 3 * hy), lambda i: (0, 0)),
            pl.BlockSpec((1, 3 * hy), lambda i: (0, 0)),
        ],
        out_specs=[row(hy), row(hy), row(fn), row(ng)],
        out_shape=[jax.ShapeDtypeStruct((t, hy), BF16), jax.ShapeDtypeStruct((t, hy), BF16),
                   jax.ShapeDtypeStruct((t, fn), BF16), jax.ShapeDtypeStruct((t, ng), BF16)],
        compiler_params=_cparams(1),
        name="proj_hyena_pre",
    )(hb, hb, hb, w_in, conv_w, conv_b.reshape(1, 3 * hy))


def _filter_kernel(tr, seq, w1_ref, b1_ref, f1_ref, w2_ref, b2_ref, f2_ref, w3_ref, dec_ref, o_ref):
    i = pl.program_id(0)
    n = i * tr + lax.broadcasted_iota(I32, (tr, 1), 0)
    m = jnp.where(n < seq, n, 2 * seq - n).astype(F32)
    t = m * (1.0 / (seq - 1))
    ang = (2.0 * math.pi / seq) * m
    band = lax.broadcasted_iota(I32, (1, FILTER_BANDS), 1).astype(F32)
    freqs = 1e-4 + band * ((FILTER_BANDS - 1 - 1e-4) / (FILTER_BANDS - 1))
    fa = ang * freqs
    hi = lax.Precision.HIGHEST
    pre = (t * w1_ref[0:1, :]
           + jnp.dot(jnp.cos(fa), w1_ref[1:1 + FILTER_BANDS, :], precision=hi, preferred_element_type=F32)
           - jnp.dot(jnp.sin(fa), w1_ref[1 + FILTER_BANDS:, :], precision=hi, preferred_element_type=F32))
    h = jnp.sin(f1_ref[...] * (pre + b1_ref[...]))
    h = jnp.sin(f2_ref[...] * (jnp.dot(h, w2_ref[...], precision=hi, preferred_element_type=F32)
                               + b2_ref[...]))
    h = jnp.dot(h, w3_ref[...], precision=hi, preferred_element_type=F32)
    h = h * (jnp.exp(-t * jnp.abs(dec_ref[...])) + MOD_SHIFT)
    o_ref[...] = jnp.where(n == seq, 0.0, h)


def _filter(seq, w1, b1, f1, w2, b2, f2, w3, dec, tr):
    hid = w1.shape[1]
    hy = w3.shape[1] // 2
    nfwd = seq // tr
    side = lambda i: (0, jnp.where(i >= nfwd, 1, 0))
    full = lambda i: (0, 0)
    return pl.pallas_call(
        functools.partial(_filter_kernel, tr, seq),
        grid=(2 * seq // tr,),
        in_specs=[
            pl.BlockSpec(w1.shape, full), pl.BlockSpec((1, hid), full), pl.BlockSpec((1, hid), full),
            pl.BlockSpec(w2.shape, full), pl.BlockSpec((1, hid), full), pl.BlockSpec((1, hid), full),
            pl.BlockSpec((hid, hy), side), pl.BlockSpec((1, hy), side),
        ],
        out_specs=pl.BlockSpec((tr, hy), lambda i: (i, 0)),
        out_shape=jax.ShapeDtypeStruct((2 * seq, hy), F32),
        compiler_params=_cparams(1),
        name="hyena_filter",
    )(w1, b1.reshape(1, hid), f1.reshape(1, hid), w2, b2.reshape(1, hid), f2.reshape(1, hid),
      w3, dec.reshape(1, 2 * hy))


def _blm_kernel(tj, shared, m_ref, x_ref, o_ref):
    for jj in range(tj):
        m = m_ref[0 if shared else jj]
        o_ref[0, jj] = jnp.dot(m, x_ref[0, jj], preferred_element_type=F32).astype(o_ref.dtype)


def _blm(m, x, tj, out_dtype=BF16):
    g, j, k, c = x.shape
    jm, mr, _ = m.shape
    shared = jm == 1
    tj = min(tj, j)
    m_spec = (pl.BlockSpec((1, mr, k), lambda jb, gb: (0, 0, 0)) if shared
              else pl.BlockSpec((tj, mr, k), lambda jb, gb: (jb, 0, 0)))
    return pl.pallas_call(
        functools.partial(_blm_kernel, tj, shared),
        grid=(j // tj, g),
        in_specs=[m_spec, pl.BlockSpec((1, tj, k, c), lambda jb, gb: (gb, jb, 0, 0))],
        out_specs=pl.BlockSpec((1, tj, mr, c), lambda jb, gb: (gb, jb, 0, 0)),
        out_shape=jax.ShapeDtypeStruct((g, j, mr, c), out_dtype),
        compiler_params=_cparams(2),
        name="dft_stage",
    )(m, x)


def _fnet_s1_kernel(tj, fn, m_ref, r_ref, x_ref, o_ref):
    for jj in range(tj):
        z = jnp.dot(x_ref[0, jj], r_ref[...], preferred_element_type=F32)
        zs = jnp.concatenate([z[:, :fn], z[:, fn:]], axis=0).astype(BF16)
        o_ref[0, jj] = jnp.dot(m_ref[jj], zs, preferred_element_type=F32).astype(o_ref.dtype)


def _fnet_s1(m, r, x, tj):
    g, j, k, fn = x.shape
    _, mr, k2 = m.shape
    tj = min(tj, j)
    return pl.pallas_call(
        functools.partial(_fnet_s1_kernel, tj, fn),
        grid=(j // tj, g),
        in_specs=[pl.BlockSpec((tj, mr, k2), lambda jb, gb: (jb, 0, 0)),
                  pl.BlockSpec(r.shape, lambda jb, gb: (0, 0)),
                  pl.BlockSpec((1, tj, k, fn), lambda jb, gb: (gb, jb, 0, 0))],
        out_specs=pl.BlockSpec((1, tj, mr, fn), lambda jb, gb: (gb, jb, 0, 0)),
        out_shape=jax.ShapeDtypeStruct((g, j, mr, fn), BF16),
        compiler_params=_cparams(2),
        name="fnet_stage1",
    )(m, r, x)


def _hyena_mid_kernel(tk, n2, f_ref, kf_ref, gi_ref, x_ref, o_ref):
    for kk in range(tk):
        a = jnp.dot(f_ref[...], x_ref[0, kk], preferred_element_type=F32)
        ar, ai = a[:n2], a[n2:]
        kr = kf_ref[kk, :n2].astype(F32)
        ki = kf_ref[kk, n2:].astype(F32)
        prod = jnp.concatenate([ar * kr - ai * ki, ar * ki + ai * kr], axis=0).astype(BF16)
        o_ref[0, kk] = jnp.dot(gi_ref[kk], prod, preferred_element_type=F32).astype(o_ref.dtype)


def _hyena_mid(f2s, kf, ginv, x, tk):
    g, n1, r, c = x.shape
    n2 = r // 2
    tk = min(tk, n1)
    return pl.pallas_call(
        functools.partial(_hyena_mid_kernel, tk, n2),
        grid=(n1 // tk, g),
        in_specs=[pl.BlockSpec((r, r), lambda kb, gb: (0, 0)),
                  pl.BlockSpec((tk, r, c), lambda kb, gb: (kb, 0, 0)),
                  pl.BlockSpec((tk, r, r), lambda kb, gb: (kb, 0, 0)),
                  pl.BlockSpec((1, tk, r, c), lambda kb, gb: (gb, kb, 0, 0))],
        out_specs=pl.BlockSpec((1, tk, r, c), lambda kb, gb: (gb, kb, 0, 0)),
        out_shape=jax.ShapeDtypeStruct(x.shape, BF16),
        compiler_params=_cparams(2),
        name="hyena_mid",
    )(f2s, kf, ginv, x)


def _cis(num, den, sign):
    ang = (2.0 * math.pi / den) * lax.rem(num, den).astype(F32)
    return jnp.cos(ang), sign * jnp.sin(ang)


def _stack_complex(mr, mi):
    top = jnp.concatenate([mr, -mi], axis=-1)
    bot = jnp.concatenate([mi, mr], axis=-1)
    return jnp.concatenate([top, bot], axis=-2)


def _iota(shape, axis):
    return lax.broadcasted_iota(I32, shape, axis)


def _hyena_mats(n, n1, n2):
    n1h = n1 // 2
    sh = (n2, n1, n1)
    gr, gi = _cis(_iota(sh, 1) * (n2 * _iota(sh, 2) + _iota(sh, 0)), n, -1.0)
    g_data = _stack_complex(gr[:, :, :n1h], gi[:, :, :n1h]).astype(BF16)
    g_filt = jnp.concatenate([gr, gi], axis=1).astype(BF16)
    sh = (n2, n2)
    fr, fi = _cis(_iota(sh, 0) * _iota(sh, 1), n2, -1.0)
    f2s = _stack_complex(fr, fi).astype(BF16)
    sh = (n1, n2, n2)
    ir, ii = _cis(_iota(sh, 1) * (_iota(sh, 0) + n1 * _iota(sh, 2)), n, 1.0)
    ginv = _stack_complex(ir, ii).astype(BF16)
    sh = (n1h, n1)
    br, bi = _cis(_iota(sh, 0) * _iota(sh, 1), n1, 1.0)
    f1inv = (_stack_complex(br, bi) * (1.0 / n)).astype(BF16)[None]
    return g_data, g_filt, f2s, ginv, f1inv


def _fnet_mats(seq, na, nb, fn):
    gd = fn // FN_GROUPS
    sh = (nb, na, na)
    gr, gi = _cis(_iota(sh, 1) * (nb * _iota(sh, 2) + _iota(sh, 0)), seq, -1.0)
    g1 = _stack_complex(gr, gi).astype(BF16)
    sh = (nb, nb)
    fr, fi = _cis(_iota(sh, 0) * _iota(sh, 1), nb, -1.0)
    scale = 1.0 / math.sqrt(seq * gd)
    f2re = (jnp.concatenate([fr, -fi], axis=-1) * scale).astype(BF16)[None]
    sh = (fn, fn)
    same = (_iota(sh, 0) // gd) == (_iota(sh, 1) // gd)
    cr, ci = _cis(_iota(sh, 0) * _iota(sh, 1), gd, -1.0)
    rmat = jnp.concatenate([jnp.where(same, cr, 0.0), jnp.where(same, ci, 0.0)], axis=1).astype(BF16)
    return g1, f2re, rmat


def _hyena_conv(vv, batch, seq, filt_args, tj):
    c = vv.shape[1]
    n = 2 * seq
    n1 = DFT_RADIX
    n2 = n // n1
    n1h = n1 // 2
    bp = batch // 2
    g_data, g_filt, f2s, ginv, f1inv = _hyena_mats(n, n1, n2)
    kt = _filter(seq, *filt_args, tr=min(512, seq))
    kt = kt.reshape(1, n1, n2, c).transpose(0, 2, 1, 3).astype(BF16)
    ks = _blm(g_filt, kt, tj)
    ks = ks.reshape(1, n2, 2, n1, c).transpose(0, 3, 2, 1, 4).reshape(1, n1, 2 * n2, c)
    kf = _blm(f2s[None], ks, tj, out_dtype=F32)[0]
    x = vv.reshape(bp, 2, n1h, n2, c).transpose(0, 3, 1, 2, 4).reshape(bp, n2, n1, c)
    a = _blm(g_data, x, tj)
    a = a.reshape(bp, n2, 2, n1, c).transpose(0, 3, 2, 1, 4).reshape(bp, n1, 2 * n2, c)
    z = _hyena_mid(f2s, kf, ginv, a, 4)
    z = z.reshape(bp, n1, 2, n2, c).transpose(0, 3, 2, 1, 4).reshape(bp, n2, 2 * n1, c)
    y = _blm(f1inv, z, tj)
    y = y.reshape(bp, n2, 2, n1h, c).transpose(0, 2, 3, 1, 4)
    return y.reshape(batch * seq, c)


def _fnet_mix(u, batch, seq, tj):
    fn = u.shape[1]
    nb = FNET_MINOR
    na = seq // nb
    g1, f2re, rmat = _fnet_mats(seq, na, nb, fn)
    x = u.reshape(batch, na, nb, fn).transpose(0, 2, 1, 3)
    a = _fnet_s1(g1, rmat, x, tj)
    a = a.reshape(batch, nb, 2, na, fn).transpose(0, 3, 2, 1, 4).reshape(batch, na, 2 * nb, fn)
    y = _blm(f2re, a, tj)
    return y.transpose(0, 2, 1, 3).reshape(batch * seq, fn)


def _merge_kernel(alpha, d, n_exp, conv_ref, vv_ref, x0_ref, yfn_ref, gate_ref, h_ref, bias_ref,
                  why_ref, wfn_ref, wo_ref, g_ref, b_ref, wr_ref, br_ref,
                  h1_ref, idx_ref, wt_ref):
    vv = vv_ref[...].astype(F32)
    y_hy = x0_ref[...].astype(F32) * (conv_ref[...].astype(F32) + vv * bias_ref[...])
    a = jnp.dot(y_hy.astype(BF16), why_ref[...], preferred_element_type=F32)
    bfn = jnp.dot(yfn_ref[...], wfn_ref[...], preferred_element_type=F32)
    gates = gate_ref[...].astype(F32)
    merged = gates[:, :d] * a + gates[:, d:] * bfn
    m = jnp.dot(merged.astype(BF16), wo_ref[...], preferred_element_type=F32)
    h1 = _layer_norm(alpha * h_ref[...] + m, g_ref[...], b_ref[...])
    h1_ref[...] = h1
    logits = jnp.dot(h1.astype(BF16), wr_ref[...], preferred_element_type=F32) + br_ref[...]
    lane = lax.broadcasted_iota(I32, logits.shape, 1)
    lane_f = lane.astype(F32)
    neg = jnp.float32(-jnp.inf)
    logits = jnp.where(lane < n_exp, logits, neg)
    vals, idxs = [], []
    for _ in range(TOP_K):
        mx = jnp.max(logits, axis=-1, keepdims=True)
        ix = jnp.min(jnp.where(logits == mx, lane_f, float(LANES)), axis=-1, keepdims=True).astype(I32)
        vals.append(mx)
        idxs.append(ix)
        logits = jnp.where(lane == ix, neg, logits)
    exps = [jnp.exp(v - vals[0]) for v in vals]
    den = exps[0]
    for e in exps[1:]:
        den = den + e
    idx_out = jnp.zeros(lane.shape, I32)
    wt_out = jnp.zeros(lane.shape, F32)
    for k in range(TOP_K):
        idx_out = jnp.where(lane == k, idxs[k], idx_out)
        wt_out = jnp.where(lane == k, exps[k] / den, wt_out)
    idx_ref[...] = idx_out
    wt_ref[...] = wt_out


def _merge(alpha, conv, vv, x0, yfn, gates, h, bias, why, wfn, wo, g, b, wr, br, tm):
    t, d = h.shape
    hy = conv.shape[1]
    fn = yfn.shape[1]
    n_exp = wr.shape[1]
    wr_p = jnp.zeros((d, LANES), BF16).at[:, :n_exp].set(wr.astype(BF16))
    br_p = jnp.zeros((1, LANES), F32).at[0, :n_exp].set(br)
    row = lambda c: pl.BlockSpec((tm, c), lambda i: (i, 0))
    full = lambda a: pl.BlockSpec(a.shape, lambda i: (0,) * a.ndim)
    args = (conv, vv, x0, yfn, gates, h, bias.reshape(1, hy), why, wfn, wo,
            g.reshape(1, d), b.reshape(1, d), wr_p, br_p)
    in_specs = [row(hy), row(hy), row(hy), row(fn), row(2 * d), row(d)] + [full(a) for a in args[6:]]
    return pl.pallas_call(
        functools.partial(_merge_kernel, alpha, d, n_exp),
        grid=(t // tm,),
        in_specs=in_specs,
        out_specs=[row(d), row(LANES), row(LANES)],
        out_shape=[jax.ShapeDtypeStruct((t, d), F32), jax.ShapeDtypeStruct((t, LANES), I32),
                   jax.ShapeDtypeStruct((t, LANES), F32)],
        compiler_params=_cparams(1),
        name="merge_route",
    )(*args)


def _rank_kernel(tm, idx_ref, rank_ref, cnt_ref, carry_ref):
    i = pl.program_id(0)

    @pl.when(i == 0)
    def _():
        carry_ref[...] = jnp.zeros_like(carry_ref)

    idx = idx_ref[...]
    lane = lax.broadcasted_iota(I32, idx.shape, 1)
    cols = [idx[:, k:k + 1] for k in range(TOP_K)]
    onehot = jnp.zeros(idx.shape, F32)
    for ck in cols:
        onehot = onehot + (lane == ck).astype(F32)
    r = lax.broadcasted_iota(I32, (tm, tm), 0)
    c = lax.broadcasted_iota(I32, (tm, tm), 1)
    tri = (c < r).astype(BF16)
    before = jnp.dot(tri, onehot.astype(BF16), preferred_element_type=F32) + carry_ref[...]
    out = jnp.zeros(idx.shape, F32)
    for k, ck in enumerate(cols):
        rk = jnp.sum(jnp.where(lane == ck, before, 0.0), axis=-1, keepdims=True)
        out = jnp.where(lane == k, rk, out)
    rank_ref[...] = out.astype(I32)
    total = carry_ref[...] + jnp.sum(onehot, axis=0, keepdims=True)
    carry_ref[...] = total
    cnt_ref[...] = total.astype(I32)


def _ranks(idx128, tm):
    t = idx128.shape[0]
    return pl.pallas_call(
        functools.partial(_rank_kernel, tm),
        grid=(t // tm,),
        in_specs=[pl.BlockSpec((tm, LANES), lambda i: (i, 0))],
        out_specs=[pl.BlockSpec((tm, LANES), lambda i: (i, 0)),
                   pl.BlockSpec((1, LANES), lambda i: (0, 0))],
        out_shape=[jax.ShapeDtypeStruct((t, LANES), I32), jax.ShapeDtypeStruct((1, LANES), I32)],
        scratch_shapes=[pltpu.VMEM((1, LANES), F32)],
        compiler_params=_cparams(1),
        name="expert_rank",
    )(idx128)


def _row_copy(src_ref, src_row, dst_ref, dst_row, sem):
    return pltpu.make_async_copy(src_ref.at[pl.ds(src_row, 1)], dst_ref.at[pl.ds(dst_row, 1)], sem)


ISSUE_UNROLL = 4
N_TILE_BUF = 3


def _dispatch_kernel(tm, n_exp, n_blocks, zstart_ref, znum_ref, nused_ref, dest_hbm, h_hbm, xs_hbm,
                     dsm, hbuf, zbuf, sem_idx, sem_tile, sem_row, sem_z):
    i = pl.program_id(0)
    n = pl.num_programs(0)
    slot = lax.rem(i, 2)
    tb = lax.rem(i, N_TILE_BUF)

    def idx_fetch(step, s):
        return pltpu.make_async_copy(dest_hbm.at[step], dsm.at[s], sem_idx.at[s])

    def tile_load(step, b):
        row = pl.multiple_of(step * tm, tm)
        return pltpu.make_async_copy(h_hbm.at[pl.ds(row, tm)], hbuf.at[b], sem_tile.at[b])

    def rows_done(s):
        return pltpu.make_async_copy(hbuf.at[0], xs_hbm.at[pl.ds(0, tm)], sem_row.at[s])

    @pl.when(i == 0)
    def _():
        idx_fetch(0, 0).start()
        tile_load(0, 0).start()

    idx_fetch(i, slot).wait()
    tile_load(i, tb).wait()

    @pl.when(i + 1 < n)
    def _():
        idx_fetch(i + 1, 1 - slot).start()
        tile_load(i + 1, lax.rem(i + 1, N_TILE_BUF)).start()

    def issue(r, carry):
        for k in range(TOP_K):
            _row_copy(hbuf.at[tb], r, xs_hbm, dsm[slot, r * TOP_K + k],
                      sem_row.at[slot]).start(priority=k % 2)
        return carry

    lax.fori_loop(0, tm, issue, 0, unroll=ISSUE_UNROLL)

    @pl.when(i > 0)
    def _():
        for _ in range(TOP_K):
            rows_done(1 - slot).wait()

    @pl.when(i == n - 1)
    def _():
        for _ in range(TOP_K):
            rows_done(slot).wait()
        zbuf[...] = jnp.zeros_like(zbuf)

        def per_expert(e, carry):
            start = zstart_ref[e]
            num = znum_ref[e]

            def zissue(r, c2):
                _row_copy(zbuf, 0, xs_hbm, start + r, sem_z).start()
                return c2

            def zwait(r, c2):
                _row_copy(zbuf, 0, xs_hbm, 0, sem_z).wait()
                return c2

            lax.fori_loop(0, num, zissue, 0)
            lax.fori_loop(0, num, zwait, 0)
            return carry

        lax.fori_loop(0, n_exp, per_expert, 0)

        def blk_copy(bi):
            row = pl.multiple_of(bi * MOE_BLOCK, MOE_BLOCK)
            return pltpu.make_async_copy(zbuf, xs_hbm.at[pl.ds(row, MOE_BLOCK)], sem_z)

        def bissue(bi, carry):
            blk_copy(bi).start()
            return carry

        def bwait(bi, carry):
            blk_copy(bi).wait()
            return carry

        lax.fori_loop(nused_ref[0], n_blocks, bissue, 0)
        lax.fori_loop(nused_ref[0], n_blocks, bwait, 0)


def _dispatch(zstart, znum, n_used, dest, h1, n_slots, tm):
    t, d = h1.shape
    n_exp = zstart.shape[0]
    dest2 = dest.reshape(t // tm, tm * TOP_K)
    return pl.pallas_call(
        functools.partial(_dispatch_kernel, tm, n_exp, n_slots // MOE_BLOCK),
        grid_spec=pltpu.PrefetchScalarGridSpec(
            num_scalar_prefetch=3,
            grid=(t // tm,),
            in_specs=[pl.BlockSpec(memory_space=pl.ANY), pl.BlockSpec(memory_space=pl.ANY)],
            out_specs=pl.BlockSpec(memory_space=pl.ANY),
            scratch_shapes=[pltpu.SMEM((2, tm * TOP_K), I32), pltpu.VMEM((N_TILE_BUF, tm, d), F32),
                            pltpu.VMEM((MOE_BLOCK, d), F32),
                            pltpu.SemaphoreType.DMA((2,)), pltpu.SemaphoreType.DMA((N_TILE_BUF,)),
                            pltpu.SemaphoreType.DMA((2,)), pltpu.SemaphoreType.DMA],
        ),
        out_shape=jax.ShapeDtypeStruct((n_slots, d), F32),
        compiler_params=_cparams(1),
        name="moe_dispatch",
    )(zstart, znum, n_used, dest2, h1)


def _expert_kernel(d_ff, be_ref, nused_ref, x_ref, wgu_ref, bgu_ref, wd_ref, bd_ref, y_ref):
    i = pl.program_id(0)

    @pl.when(i < nused_ref[0])
    def _():
        x = x_ref[...].astype(BF16)
        gu = jnp.dot(x, wgu_ref[0], preferred_element_type=F32) + bgu_ref[0]
        gate = jnp.minimum(gu[:, :d_ff], SWIGLU_LIMIT)
        up = jnp.clip(gu[:, d_ff:], -SWIGLU_LIMIT, SWIGLU_LIMIT)
        act = gate * jax.nn.sigmoid(SWIGLU_ALPHA * gate) * (up + 1.0)
        y_ref[...] = jnp.dot(act.astype(BF16), wd_ref[0], preferred_element_type=F32) + bd_ref[0]

    @pl.when(i >= nused_ref[0])
    def _():
        y_ref[...] = jnp.zeros_like(y_ref)


def _experts(block_e, n_used, xs, wgu, bgu, wd, bd):
    n_slots, d = xs.shape
    n_exp, _, two_ff = wgu.shape
    d_ff = two_ff // 2
    nb = n_slots // MOE_BLOCK
    live = lambda i, be, nu: jnp.minimum(i, nu[0] - 1)
    return pl.pallas_call(
        functools.partial(_expert_kernel, d_ff),
        grid_spec=pltpu.PrefetchScalarGridSpec(
            num_scalar_prefetch=2,
            grid=(nb,),
            in_specs=[
                pl.BlockSpec((MOE_BLOCK, d), lambda i, be, nu: (live(i, be, nu), 0)),
                pl.BlockSpec((1, d, two_ff), lambda i, be, nu: (be[live(i, be, nu)], 0, 0)),
                pl.BlockSpec((1, 1, two_ff), lambda i, be, nu: (be[live(i, be, nu)], 0, 0)),
                pl.BlockSpec((1, d_ff, d), lambda i, be, nu: (be[live(i, be, nu)], 0, 0)),
                pl.BlockSpec((1, 1, d), lambda i, be, nu: (be[live(i, be, nu)], 0, 0)),
            ],
            out_specs=pl.BlockSpec((MOE_BLOCK, d), lambda i, be, nu: (i, 0)),
        ),
        out_shape=jax.ShapeDtypeStruct((n_slots, d), F32),
        compiler_params=_cparams(1),
        name="moe_experts",
    )(block_e, n_used, xs, wgu, bgu.reshape(n_exp, 1, two_ff), wd, bd.reshape(n_exp, 1, d))


def _combine_kernel(tm, alpha, n_first, split, dest_hbm, h1_ref, wt_ref, g_ref, b_ref, y_hbm, *rest):
    if split:
        oa_ref, ob_ref, dsm, ybuf, sem_idx, sem = rest
    else:
        o_ref, ob16_ref, dsm, ybuf, sem_idx, sem = rest
    i = pl.program_id(0)
    n = pl.num_programs(0)
    slot = lax.rem(i, 2)

    def idx_fetch(step, s):
        return pltpu.make_async_copy(dest_hbm.at[step], dsm.at[s], sem_idx.at[s])

    def gather(s):
        def issue(r, carry):
            for k in range(TOP_K):
                _row_copy(y_hbm, dsm[s, r * TOP_K + k], ybuf.at[s, k], r,
                          sem.at[s]).start(priority=k % 2)
            return carry

        lax.fori_loop(0, tm, issue, 0, unroll=ISSUE_UNROLL)

    @pl.when(i == 0)
    def _():
        first = idx_fetch(0, 0)
        first.start()
        first.wait()
        gather(0)

        @pl.when(n > 1)
        def _():
            idx_fetch(1, 1).start()

    @pl.when(i + 1 < n)
    def _():
        idx_fetch(i + 1, 1 - slot).wait()
        gather(1 - slot)

    @pl.when(i + 2 < n)
    def _():
        idx_fetch(i + 2, slot).start()

    for k in range(TOP_K):
        pltpu.make_async_copy(y_hbm.at[pl.ds(0, tm)], ybuf.at[slot, k], sem.at[slot]).wait()
    wt = wt_ref[...]
    f = wt[:, 0:1] * ybuf[slot, 0]
    for k in range(1, TOP_K):
        f = f + wt[:, k:k + 1] * ybuf[slot, k]
    out = _layer_norm(alpha * h1_ref[...] + f, g_ref[...], b_ref[...])
    if split:
        @pl.when(i < n_first)
        def _():
            oa_ref[...] = out

        @pl.when(i >= n_first)
        def _():
            ob_ref[...] = out
    else:
        o_ref[...] = out
        ob16_ref[...] = out.astype(BF16)


def _combine(alpha, dest, h1, wt128, g, b, y, tm, split_rows=None):
    t, d = h1.shape
    nsteps = t // tm
    dest2 = dest.reshape(nsteps, tm * TOP_K)
    split = split_rows is not None
    row = lambda c: pl.BlockSpec((tm, c), lambda i: (i, 0))
    vec = pl.BlockSpec((1, d), lambda i: (0, 0))
    if split:
        na = split_rows // tm
        out_specs = [pl.BlockSpec((tm, d), lambda i: (jnp.minimum(i, na - 1), 0)),
                     pl.BlockSpec((tm, d), lambda i: (jnp.maximum(i - na, 0), 0))]
        out_shape = [jax.ShapeDtypeStruct((split_rows, d), F32),
                     jax.ShapeDtypeStruct((t - split_rows, d), F32)]
    else:
        na = 0
        out_specs = [row(d), row(d)]
        out_shape = [jax.ShapeDtypeStruct((t, d), F32), jax.ShapeDtypeStruct((t, d), BF16)]
    return pl.pallas_call(
        functools.partial(_combine_kernel, tm, alpha, na, split),
        grid=(nsteps,),
        in_specs=[pl.BlockSpec(memory_space=pl.ANY), row(d), row(LANES), vec, vec,
                  pl.BlockSpec(memory_space=pl.ANY)],
        out_specs=out_specs,
        out_shape=out_shape,
        scratch_shapes=[pltpu.SMEM((2, tm * TOP_K), I32), pltpu.VMEM((2, TOP_K, tm, d), F32),
                        pltpu.SemaphoreType.DMA((2,)), pltpu.SemaphoreType.DMA((2,))],
        compiler_params=_cparams(1),
        name="moe_combine",
    )(dest2, h1, wt128, g.reshape(1, d), b.reshape(1, d), y)


def _route_plan(idx, rank, counts, n_blocks):
    padded = (counts + MOE_BLOCK - 1) // MOE_BLOCK * MOE_BLOCK
    pad_ends = jnp.cumsum(padded)
    pad_starts = pad_ends - padded
    dest = pad_starts[idx] + rank
    block_start = jnp.arange(n_blocks, dtype=I32) * MOE_BLOCK
    n_exp = counts.shape[0]
    block_e = jnp.minimum(jnp.sum(block_start[:, None] >= pad_ends[None, :], axis=1), n_exp - 1).astype(I32)
    n_used = (pad_ends[-1] // MOE_BLOCK).astype(I32).reshape(1)
    return dest.astype(I32), block_e, n_used, (pad_starts + counts).astype(I32), (padded - counts).astype(I32)


def _tile(n, pref):
    t = min(pref, n)
    while n % t:
        t //= 2
    return t


def kernel(x_prompt, x_sample, ln_in_g, ln_in_b, w_in, conv_w, conv_b, filt_w1, filt_b1, filt_freq1, filt_w2, filt_b2, filt_freq2, filt_w3, filt_decay, hy_bias, w_hy_out, w_fn_out, w_o, ln1_g, ln1_b, w_router, b_router, w_gu, b_gu, w_down, b_down, ln2_g, ln2_b):
    bp, lp, d = x_prompt.shape
    bs, ls, _ = x_sample.shape
    depth = w_in.shape[0]
    hy = hy_bias.shape[-1]
    fn = w_fn_out.shape[1]
    n_exp = w_router.shape[-1]
    tp, ts = bp * lp, bs * ls
    t = tp + ts
    alpha = (2 * depth) ** 0.25
    trunks = ((0, tp, bp, lp), (tp, ts, bs, ls))
    n_blocks = -(-(t * TOP_K) // MOE_BLOCK) + n_exp
    n_slots = n_blocks * MOE_BLOCK

    tm_ln = _tile(math.gcd(tp, ts), 1024)
    tm_tok = _tile(math.gcd(lp, ls), 512)
    tm_row = _tile(math.gcd(tp, ts), 256)

    h, hb = _ln_in(x_prompt.reshape(tp, d), x_sample.reshape(ts, d), ln_in_g, ln_in_b, tm_ln)
    out = None
    for l in range(depth):
        x0, vv, z_fn, gates = _proj(hb, w_in[l].astype(BF16), conv_w[l], conv_b[l], hy, fn,
                                    tp, lp, ls, tm_tok)
        filt_args = (filt_w1[l], filt_b1[l], filt_freq1[l], filt_w2[l], filt_b2[l], filt_freq2[l],
                     filt_w3[l], filt_decay[l])
        conv = jnp.concatenate(
            [_hyena_conv(vv[o:o + n], b, s, filt_args, 8) for (o, n, b, s) in trunks], axis=0)
        yfn = jnp.concatenate(
            [_fnet_mix(z_fn[o:o + n], b, s, 8) for (o, n, b, s) in trunks], axis=0)
        h1, idx128, wt128 = _merge(alpha, conv, vv, x0, yfn, gates, h, hy_bias[l],
                                   w_hy_out[l].astype(BF16), w_fn_out[l].astype(BF16),
                                   w_o[l].astype(BF16), ln1_g[l], ln1_b[l], w_router[l], b_router[l],
                                   tm_tok)
        rank128, cnt128 = _ranks(idx128, tm_tok)
        dest, block_e, n_used, zstart, znum = _route_plan(
            idx128[:, :TOP_K], rank128[:, :TOP_K], cnt128[0, :n_exp], n_blocks)
        xs = _dispatch(zstart, znum, n_used, dest, h1, n_slots, tm_row)
        y = _experts(block_e, n_used, xs, w_gu[l].astype(BF16), b_gu[l], w_down[l].astype(BF16),
                     b_down[l])
        if l + 1 < depth:
            h, hb = _combine(alpha, dest, h1, wt128, ln2_g[l], ln2_b[l], y, tm_row)
        else:
            out = _combine(alpha, dest, h1, wt128, ln2_g[l], ln2_b[l], y, tm_row, split_rows=tp)
    return (out[0].reshape(bp, lp, d), out[1].reshape(bs, ls, d))
```

```python
import functools
import math

import jax
import jax.numpy as jnp
from jax import lax
from jax.experimental import pallas as pl
from jax.experimental.pallas import tpu as pltpu

F32 = jnp.float32
BF16 = jnp.bfloat16
I32 = jnp.int32

TOP_K = 4
FN_GROUPS = 4
MOE_BLOCK = 512
FILTER_BANDS = 16
MOD_SHIFT = 0.05
SWIGLU_LIMIT = 7.0
SWIGLU_ALPHA = 1.702
LN_EPS = 1e-5
LANES = 128
DFT_RADIX = 128
FNET_MINOR = 64
VMEM_LIMIT = 48 * 1024 * 1024


def _cparams(ndim, vmem=VMEM_LIMIT):
    return pltpu.CompilerParams(dimension_semantics=("arbitrary",) * ndim,
                                vmem_limit_bytes=vmem)


def _store_token_tiles(ref, x):
    rows, d = x.shape
    nch = d // LANES
    for c in range(nch):
        ref[pl.ds(c, rows, stride=nch), :] = x[:, c * LANES:(c + 1) * LANES]


def _load_token_tiles(ref, rows, nch):
    return jnp.concatenate([ref[pl.ds(c, rows, stride=nch), :] for c in range(nch)], axis=1)


def _layer_norm(x, g, b):
    mu = jnp.mean(x, axis=-1, keepdims=True)
    xc = x - mu
    var = jnp.mean(xc * xc, axis=-1, keepdims=True)
    return xc * lax.rsqrt(var + LN_EPS) * g + b


def _ln_in_kernel(n_first, xa_ref, xb_ref, g_ref, b_ref, h_ref, hb_ref):
    i = pl.program_id(0)

    def emit(x_ref):
        y = _layer_norm(x_ref[...], g_ref[...], b_ref[...])
        h_ref[...] = y
        hb_ref[...] = y.astype(BF16)

    @pl.when(i < n_first)
    def _():
        emit(xa_ref)

    @pl.when(i >= n_first)
    def _():
        emit(xb_ref)


def _ln_in(xa, xb, g, b, tm):
    ta, d = xa.shape
    tb = xb.shape[0]
    na, nb = ta // tm, tb // tm
    t = ta + tb
    return pl.pallas_call(
        functools.partial(_ln_in_kernel, na),
        grid=(na + nb,),
        in_specs=[
            pl.BlockSpec((tm, d), lambda i: (jnp.minimum(i, na - 1), 0)),
            pl.BlockSpec((tm, d), lambda i: (jnp.maximum(i - na, 0), 0)),
            pl.BlockSpec((1, d), lambda i: (0, 0)),
            pl.BlockSpec((1, d), lambda i: (0, 0)),
        ],
        out_specs=[pl.BlockSpec((tm, d), lambda i: (i, 0)),
                   pl.BlockSpec((tm, d), lambda i: (i, 0))],
        out_shape=[jax.ShapeDtypeStruct((t, d), F32), jax.ShapeDtypeStruct((t, d), BF16)],
        compiler_params=_cparams(1),
        name="ln_in",
    )(xa, xb, g.reshape(1, d), b.reshape(1, d))


HALO = 16


def _proj_kernel(tm, t_first, l_first, l_second, hy, fn, x_ref, xp_ref, xn_ref, w_ref, cw_ref, cb_ref,
                 x0_ref, vv_ref, zfn_ref, gate_ref):
    i = pl.program_id(0)
    r0 = i * tm
    seq = jnp.where(r0 < t_first, l_first, l_second)
    is_first = lax.rem(r0, seq) == 0
    is_last = lax.rem(r0 + tm, seq) == 0
    x = x_ref[...]
    x_ext = jnp.concatenate([xp_ref[...], x, xn_ref[...]], axis=0)
    row = lax.broadcasted_iota(I32, (tm, 1), 0)
    kill_prev = jnp.logical_and(row == 0, is_first)
    kill_next = jnp.logical_and(row == tm - 1, is_last)

    def conv_chunk(c):
        lo, hi = c * hy, (c + 1) * hy
        z = jnp.dot(x_ext, w_ref[:, lo:hi], preferred_element_type=F32)
        um1 = jnp.where(kill_prev, 0.0, z[HALO - 1:HALO - 1 + tm])
        up1 = jnp.where(kill_next, 0.0, z[HALO + 1:HALO + 1 + tm])
        return (um1 * cw_ref[0:1, lo:hi] + z[HALO:HALO + tm] * cw_ref[1:2, lo:hi]
                + up1 * cw_ref[2:3, lo:hi] + cb_ref[:, lo:hi])

    x0_ref[...] = conv_chunk(0).astype(BF16)
    vv_ref[...] = (conv_chunk(2) * conv_chunk(1)).astype(BF16)
    zfn_ref[...] = jnp.dot(x, w_ref[:, 3 * hy:3 * hy + fn], preferred_element_type=F32).astype(BF16)
    g = jnp.dot(x, w_ref[:, 3 * hy + fn:], preferred_element_type=F32)
    gate_ref[...] = jax.nn.sigmoid(g).astype(BF16)


def _proj(hb, w_in, conv_w, conv_b, hy, fn, t_first, l_first, l_second, tm):
    t, d = hb.shape
    n = w_in.shape[1]
    ng = n - 3 * hy - fn
    nh = tm // HALO
    last = t // HALO - 1
    row = lambda c: pl.BlockSpec((tm, c), lambda i: (i, 0))
    return pl.pallas_call(
        functools.partial(_proj_kernel, tm, t_first, l_first, l_second, hy, fn),
        grid=(t // tm,),
        in_specs=[
            row(d),
            pl.BlockSpec((HALO, d), lambda i: (jnp.maximum(i * nh - 1, 0), 0)),
            pl.BlockSpec((HALO, d), lambda i: (jnp.minimum((i + 1) * nh, last), 0)),
            pl.BlockSpec((d, n), lambda i: (0, 0)),
            pl.BlockSpec((3, 3 * hy), lambda i: (0, 0)),
            pl.BlockSpec((1, 3 * hy), lambda i: (0, 0)),
        ],
        out_specs=[row(hy), row(hy), row(fn), row(ng)],
        out_shape=[jax.ShapeDtypeStruct((t, hy), BF16), jax.ShapeDtypeStruct((t, hy), BF16),
                   jax.ShapeDtypeStruct((t, fn), BF16), jax.ShapeDtypeStruct((t, ng), BF16)],
        compiler_params=_cparams(1),
        name="proj_hyena_pre",
    )(hb, hb, hb, w_in, conv_w, conv_b.reshape(1, 3 * hy))


def _filter_kernel(tr, seq, w1_ref, b1_ref, f1_ref, w2_ref, b2_ref, f2_ref, w3_ref, dec_ref, o_ref):
    i = pl.program_id(0)
    n = i * tr + lax.broadcasted_iota(I32, (tr, 1), 0)
    m = jnp.where(n < seq, n, 2 * seq - n).astype(F32)
    t = m * (1.0 / (seq - 1))
    ang = (2.0 * math.pi / seq) * m
    band = lax.broadcasted_iota(I32, (1, FILTER_BANDS), 1).astype(F32)
    freqs = 1e-4 + band * ((FILTER_BANDS - 1 - 1e-4) / (FILTER_BANDS - 1))
    fa = ang * freqs
    hi = lax.Precision.HIGHEST
    pre = (t * w1_ref[0:1, :]
           + jnp.dot(jnp.cos(fa), w1_ref[1:1 + FILTER_BANDS, :], precision=hi, preferred_element_type=F32)
           - jnp.dot(jnp.sin(fa), w1_ref[1 + FILTER_BANDS:, :], precision=hi, preferred_element_type=F32))
    h = jnp.sin(f1_ref[...] * (pre + b1_ref[...]))
    h = jnp.sin(f2_ref[...] * (jnp.dot(h, w2_ref[...], precision=hi, preferred_element_type=F32)
                               + b2_ref[...]))
    h = jnp.dot(h, w3_ref[...], precision=hi, preferred_element_type=F32)
    h = h * (jnp.exp(-t * jnp.abs(dec_ref[...])) + MOD_SHIFT)
    o_ref[...] = jnp.where(n == seq, 0.0, h)


def _filter(seq, w1, b1, f1, w2, b2, f2, w3, dec, tr):
    hid = w1.shape[1]
    hy = w3.shape[1] // 2
    nfwd = seq // tr
    side = lambda i: (0, jnp.where(i >= nfwd, 1, 0))
    full = lambda i: (0, 0)
    return pl.pallas_call(
        functools.partial(_filter_kernel, tr, seq),
        grid=(2 * seq // tr,),
        in_specs=[
            pl.BlockSpec(w1.shape, full), pl.BlockSpec((1, hid), full), pl.BlockSpec((1, hid), full),
            pl.BlockSpec(w2.shape, full), pl.BlockSpec((1, hid), full), pl.BlockSpec((1, hid), full),
            pl.BlockSpec((hid, hy), side), pl.BlockSpec((1, hy), side),
        ],
        out_specs=pl.BlockSpec((tr, hy), lambda i: (i, 0)),
        out_shape=jax.ShapeDtypeStruct((2 * seq, hy), F32),
        compiler_params=_cparams(1),
        name="hyena_filter",
    )(w1, b1.reshape(1, hid), f1.reshape(1, hid), w2, b2.reshape(1, hid), f2.reshape(1, hid),
      w3, dec.reshape(1, 2 * hy))


def _blm_kernel(tj, shared, m_ref, x_ref, o_ref):
    for jj in range(tj):
        m = m_ref[0 if shared else jj]
        o_ref[0, jj] = jnp.dot(m, x_ref[0, jj], preferred_element_type=F32).astype(o_ref.dtype)


def _blm(m, x, tj, out_dtype=BF16):
    g, j, k, c = x.shape
    jm, mr, _ = m.shape
    shared = jm == 1
    tj = min(tj, j)
    m_spec = (pl.BlockSpec((1, mr, k), lambda jb, gb: (0, 0, 0)) if shared
              else pl.BlockSpec((tj, mr, k), lambda jb, gb: (jb, 0, 0)))
    return pl.pallas_call(
        functools.partial(_blm_kernel, tj, shared),
        grid=(j // tj, g),
        in_specs=[m_spec, pl.BlockSpec((1, tj, k, c), lambda jb, gb: (gb, jb, 0, 0))],
        out_specs=pl.BlockSpec((1, tj, mr, c), lambda jb, gb: (gb, jb, 0, 0)),
        out_shape=jax.ShapeDtypeStruct((g, j, mr, c), out_dtype),
        compiler_params=_cparams(2),
        name="dft_stage",
    )(m, x)


def _fnet_s1_kernel(tj, fn, m_ref, r_ref, x_ref, o_ref):
    for jj in range(tj):
        z = jnp.dot(x_ref[0, jj], r_ref[...], preferred_element_type=F32)
        zs = jnp.concatenate([z[:, :fn], z[:, fn:]], axis=0).astype(BF16)
        o_ref[0, jj] = jnp.dot(m_ref[jj], zs, preferred_element_type=F32).astype(o_ref.dtype)


def _fnet_s1(m, r, x, tj):
    g, j, k, fn = x.shape
    _, mr, k2 = m.shape
    tj = min(tj, j)
    return pl.pallas_call(
        functools.partial(_fnet_s1_kernel, tj, fn),
        grid=(j // tj, g),
        in_specs=[pl.BlockSpec((tj, mr, k2), lambda jb, gb: (jb, 0, 0)),
                  pl.BlockSpec(r.shape, lambda jb, gb: (0, 0)),
                  pl.BlockSpec((1, tj, k, fn), lambda jb, gb: (gb, jb, 0, 0))],
        out_specs=pl.BlockSpec((1, tj, mr, fn), lambda jb, gb: (gb, jb, 0, 0)),
        out_shape=jax.ShapeDtypeStruct((g, j, mr, fn), BF16),
        compiler_params=_cparams(2),
        name="fnet_stage1",
    )(m, r, x)


def _hyena_mid_kernel(tk, n2, f_ref, kf_ref, gi_ref, x_ref, o_ref):
    for kk in range(tk):
        a = jnp.dot(f_ref[...], x_ref[0, kk], preferred_element_type=F32)
        ar, ai = a[:n2], a[n2:]
        kr = kf_ref[kk, :n2].astype(F32)
        ki = kf_ref[kk, n2:].astype(F32)
        prod = jnp.concatenate([ar * kr - ai * ki, ar * ki + ai * kr], axis=0).astype(BF16)
        o_ref[0, kk] = jnp.dot(gi_ref[kk], prod, preferred_element_type=F32).astype(o_ref.dtype)


def _hyena_mid(f2s, kf, ginv, x, tk):
    g, n1, r, c = x.shape
    n2 = r // 2
    tk = min(tk, n1)
    return pl.pallas_call(
        functools.partial(_hyena_mid_kernel, tk, n2),
        grid=(n1 // tk, g),
        in_specs=[pl.BlockSpec((r, r), lambda kb, gb: (0, 0)),
                  pl.BlockSpec((tk, r, c), lambda kb, gb: (kb, 0, 0)),
                  pl.BlockSpec((tk, r, r), lambda kb, gb: (kb, 0, 0)),
                  pl.BlockSpec((1, tk, r, c), lambda kb, gb: (gb, kb, 0, 0))],
        out_specs=pl.BlockSpec((1, tk, r, c), lambda kb, gb: (gb, kb, 0, 0)),
        out_shape=jax.ShapeDtypeStruct(x.shape, BF16),
        compiler_params=_cparams(2),
        name="hyena_mid",
    )(f2s, kf, ginv, x)


def _cis(num, den, sign):
    ang = (2.0 * math.pi / den) * lax.rem(num, den).astype(F32)
    return jnp.cos(ang), sign * jnp.sin(ang)


def _stack_complex(mr, mi):
    top = jnp.concatenate([mr, -mi], axis=-1)
    bot = jnp.concatenate([mi, mr], axis=-1)
    return jnp.concatenate([top, bot], axis=-2)


def _iota(shape, axis):
    return lax.broadcasted_iota(I32, shape, axis)


def _hyena_mats(n, n1, n2):
    n1h = n1 // 2
    sh = (n2, n1, n1)
    gr, gi = _cis(_iota(sh, 1) * (n2 * _iota(sh, 2) + _iota(sh, 0)), n, -1.0)
    g_data = _stack_complex(gr[:, :, :n1h], gi[:, :, :n1h]).astype(BF16)
    g_filt = jnp.concatenate([gr, gi], axis=1).astype(BF16)
    sh = (n2, n2)
    fr, fi = _cis(_iota(sh, 0) * _iota(sh, 1), n2, -1.0)
    f2s = _stack_complex(fr, fi).astype(BF16)
    sh = (n1, n2, n2)
    ir, ii = _cis(_iota(sh, 1) * (_iota(sh, 0) + n1 * _iota(sh, 2)), n, 1.0)
    ginv = _stack_complex(ir, ii).astype(BF16)
    sh = (n1h, n1)
    br, bi = _cis(_iota(sh, 0) * _iota(sh, 1), n1, 1.0)
    f1inv = (_stack_complex(br, bi) * (1.0 / n)).astype(BF16)[None]
    return g_data, g_filt, f2s, ginv, f1inv


def _fnet_mats(seq, na, nb, fn):
    gd = fn // FN_GROUPS
    sh = (nb, na, na)
    gr, gi = _cis(_iota(sh, 1) * (nb * _iota(sh, 2) + _iota(sh, 0)), seq, -1.0)
    g1 = _stack_complex(gr, gi).astype(BF16)
    sh = (nb, nb)
    fr, fi = _cis(_iota(sh, 0) * _iota(sh, 1), nb, -1.0)
    scale = 1.0 / math.sqrt(seq * gd)
    f2re = (jnp.concatenate([fr, -fi], axis=-1) * scale).astype(BF16)[None]
    sh = (fn, fn)
    same = (_iota(sh, 0) // gd) == (_iota(sh, 1) // gd)
    cr, ci = _cis(_iota(sh, 0) * _iota(sh, 1), gd, -1.0)
    rmat = jnp.concatenate([jnp.where(same, cr, 0.0), jnp.where(same, ci, 0.0)], axis=1).astype(BF16)
    return g1, f2re, rmat


def _hyena_conv(vv, batch, seq, filt_args, tj):
    c = vv.shape[1]
    n = 2 * seq
    n1 = DFT_RADIX
    n2 = n // n1
    n1h = n1 // 2
    bp = batch // 2
    g_data, g_filt, f2s, ginv, f1inv = _hyena_mats(n, n1, n2)
    kt = _filter(seq, *filt_args, tr=min(512, seq))
    kt = kt.reshape(1, n1, n2, c).transpose(0, 2, 1, 3).astype(BF16)
    ks = _blm(g_filt, kt, tj)
    ks = ks.reshape(1, n2, 2, n1, c).transpose(0, 3, 2, 1, 4).reshape(1, n1, 2 * n2, c)
    kf = _blm(f2s[None], ks, tj, out_dtype=F32)[0]
    x = vv.reshape(bp, 2, n1h, n2, c).transpose(0, 3, 1, 2, 4).reshape(bp, n2, n1, c)
    a = _blm(g_data, x, tj)
    a = a.reshape(bp, n2, 2, n1, c).transpose(0, 3, 2, 1, 4).reshape(bp, n1, 2 * n2, c)
    z = _hyena_mid(f2s, kf, ginv, a, 4)
    z = z.reshape(bp, n1, 2, n2, c).transpose(0, 3, 2, 1, 4).reshape(bp, n2, 2 * n1, c)
    y = _blm(f1inv, z, tj)
    y = y.reshape(bp, n2, 2, n1h, c).transpose(0, 2, 3, 1, 4)
    return y.reshape(batch * seq, c)


def _fnet_mix(u, batch, seq, tj):
    fn = u.shape[1]
    nb = FNET_MINOR
    na = seq // nb
    g1, f2re, rmat = _fnet_mats(seq, na, nb, fn)
    x = u.reshape(batch, na, nb, fn).transpose(0, 2, 1, 3)
    a = _fnet_s1(g1, rmat, x, tj)
    a = a.reshape(batch, nb, 2, na, fn).transpose(0, 3, 2, 1, 4).reshape(batch, na, 2 * nb, fn)
    y = _blm(f2re, a, tj)
    return y.transpose(0, 2, 1, 3).reshape(batch * seq, fn)


def _merge_kernel(alpha, d, n_exp, conv_ref, vv_ref, x0_ref, yfn_ref, gate_ref, h_ref, bias_ref,
                  why_ref, wfn_ref, wo_ref, g_ref, b_ref, wr_ref, br_ref,
                  h1_ref, idx_ref, wt_ref):
    vv = vv_ref[...].astype(F32)
    y_hy = x0_ref[...].astype(F32) * (conv_ref[...].astype(F32) + vv * bias_ref[...])
    a = jnp.dot(y_hy.astype(BF16), why_ref[...], preferred_element_type=F32)
    bfn = jnp.dot(yfn_ref[...], wfn_ref[...], preferred_element_type=F32)
    gates = gate_ref[...].astype(F32)
    merged = gates[:, :d] * a + gates[:, d:] * bfn
    m = jnp.dot(merged.astype(BF16), wo_ref[...], preferred_element_type=F32)
    h1 = _layer_norm(alpha * h_ref[...] + m, g_ref[...], b_ref[...])
    _store_token_tiles(h1_ref, h1)
    logits = jnp.dot(h1.astype(BF16), wr_ref[...], preferred_element_type=F32) + br_ref[...]
    lane = lax.broadcasted_iota(I32, logits.shape, 1)
    lane_f = lane.astype(F32)
    neg = jnp.float32(-jnp.inf)
    logits = jnp.where(lane < n_exp, logits, neg)
    vals, idxs = [], []
    for _ in range(TOP_K):
        mx = jnp.max(logits, axis=-1, keepdims=True)
        ix = jnp.min(jnp.where(logits == mx, lane_f, float(LANES)), axis=-1, keepdims=True).astype(I32)
        vals.append(mx)
        idxs.append(ix)
        logits = jnp.where(lane == ix, neg, logits)
    exps = [jnp.exp(v - vals[0]) for v in vals]
    den = exps[0]
    for e in exps[1:]:
        den = den + e
    idx_out = jnp.zeros(lane.shape, I32)
    wt_out = jnp.zeros(lane.shape, F32)
    for k in range(TOP_K):
        idx_out = jnp.where(lane == k, idxs[k], idx_out)
        wt_out = jnp.where(lane == k, exps[k] / den, wt_out)
    idx_ref[...] = idx_out
    wt_ref[...] = wt_out


def _merge(alpha, conv, vv, x0, yfn, gates, h, bias, why, wfn, wo, g, b, wr, br, tm):
    t, d = h.shape
    hy = conv.shape[1]
    fn = yfn.shape[1]
    n_exp = wr.shape[1]
    wr_p = jnp.zeros((d, LANES), BF16).at[:, :n_exp].set(wr.astype(BF16))
    br_p = jnp.zeros((1, LANES), F32).at[0, :n_exp].set(br)
    row = lambda c: pl.BlockSpec((tm, c), lambda i: (i, 0))
    full = lambda a: pl.BlockSpec(a.shape, lambda i: (0,) * a.ndim)
    args = (conv, vv, x0, yfn, gates, h, bias.reshape(1, hy), why, wfn, wo,
            g.reshape(1, d), b.reshape(1, d), wr_p, br_p)
    in_specs = [row(hy), row(hy), row(hy), row(fn), row(2 * d), row(d)] + [full(a) for a in args[6:]]
    return pl.pallas_call(
        functools.partial(_merge_kernel, alpha, d, n_exp),
        grid=(t // tm,),
        in_specs=in_specs,
        out_specs=[pl.BlockSpec((tm * (d // LANES), LANES), lambda i: (i, 0)), row(LANES), row(LANES)],
        out_shape=[jax.ShapeDtypeStruct((t * (d // LANES), LANES), F32),
                   jax.ShapeDtypeStruct((t, LANES), I32), jax.ShapeDtypeStruct((t, LANES), F32)],
        compiler_params=_cparams(1),
        name="merge_route",
    )(*args)


def _rank_kernel(tm, idx_ref, rank_ref, cnt_ref, carry_ref):
    i = pl.program_id(0)

    @pl.when(i == 0)
    def _():
        carry_ref[...] = jnp.zeros_like(carry_ref)

    idx = idx_ref[...]
    lane = lax.broadcasted_iota(I32, idx.shape, 1)
    cols = [idx[:, k:k + 1] for k in range(TOP_K)]
    onehot = jnp.zeros(idx.shape, F32)
    for ck in cols:
        onehot = onehot + (lane == ck).astype(F32)
    r = lax.broadcasted_iota(I32, (tm, tm), 0)
    c = lax.broadcasted_iota(I32, (tm, tm), 1)
    tri = (c < r).astype(BF16)
    before = jnp.dot(tri, onehot.astype(BF16), preferred_element_type=F32) + carry_ref[...]
    out = jnp.zeros(idx.shape, F32)
    for k, ck in enumerate(cols):
        rk = jnp.sum(jnp.where(lane == ck, before, 0.0), axis=-1, keepdims=True)
        out = jnp.where(lane == k, rk, out)
    rank_ref[...] = out.astype(I32)
    total = carry_ref[...] + jnp.sum(onehot, axis=0, keepdims=True)
    carry_ref[...] = total
    cnt_ref[...] = total.astype(I32)


def _ranks(idx128, tm):
    t = idx128.shape[0]
    return pl.pallas_call(
        functools.partial(_rank_kernel, tm),
        grid=(t // tm,),
        in_specs=[pl.BlockSpec((tm, LANES), lambda i: (i, 0))],
        out_specs=[pl.BlockSpec((tm, LANES), lambda i: (i, 0)),
                   pl.BlockSpec((1, LANES), lambda i: (0, 0))],
        out_shape=[jax.ShapeDtypeStruct((t, LANES), I32), jax.ShapeDtypeStruct((1, LANES), I32)],
        scratch_shapes=[pltpu.VMEM((1, LANES), F32)],
        compiler_params=_cparams(1),
        name="expert_rank",
    )(idx128)


def _token_copy(nch, src_ref, src_tok, dst_ref, dst_tok, sem):
    src = pl.ds(pl.multiple_of(src_tok * nch, nch), nch)
    dst = pl.ds(pl.multiple_of(dst_tok * nch, nch), nch)
    return pltpu.make_async_copy(src_ref.at[src], dst_ref.at[dst], sem)


ISSUE_UNROLL = 4
N_TILE_BUF = 3


def _dispatch_kernel(tm, nch, n_exp, n_blocks, zstart_ref, znum_ref, nused_ref, dest_hbm, h_hbm, xs_hbm,
                     dsm, hbuf, zbuf, sem_idx, sem_tile, sem_row, sem_z):
    i = pl.program_id(0)
    n = pl.num_programs(0)
    slot = lax.rem(i, 2)
    tb = lax.rem(i, N_TILE_BUF)
    tile_rows = tm * nch
    blk_rows = MOE_BLOCK * nch

    def idx_fetch(step, s):
        return pltpu.make_async_copy(dest_hbm.at[step], dsm.at[s], sem_idx.at[s])

    def tile_load(step, b):
        row = pl.multiple_of(step * tile_rows, tile_rows)
        return pltpu.make_async_copy(h_hbm.at[pl.ds(row, tile_rows)], hbuf.at[b], sem_tile.at[b])

    def rows_done(s):
        return pltpu.make_async_copy(hbuf.at[0], xs_hbm.at[pl.ds(0, tile_rows)], sem_row.at[s])

    @pl.when(i == 0)
    def _():
        idx_fetch(0, 0).start()
        tile_load(0, 0).start()

    idx_fetch(i, slot).wait()
    tile_load(i, tb).wait()

    @pl.when(i + 1 < n)
    def _():
        idx_fetch(i + 1, 1 - slot).start()
        tile_load(i + 1, lax.rem(i + 1, N_TILE_BUF)).start()

    def issue(r, carry):
        for k in range(TOP_K):
            _token_copy(nch, hbuf.at[tb], r, xs_hbm, dsm[slot, r * TOP_K + k],
                        sem_row.at[slot]).start(priority=k % 2)
        return carry

    lax.fori_loop(0, tm, issue, 0, unroll=ISSUE_UNROLL)

    @pl.when(i > 0)
    def _():
        for _ in range(TOP_K):
            rows_done(1 - slot).wait()

    @pl.when(i == n - 1)
    def _():
        for _ in range(TOP_K):
            rows_done(slot).wait()
        zbuf[...] = jnp.zeros_like(zbuf)

        def per_expert(e, carry):
            start = zstart_ref[e]
            num = znum_ref[e]

            def zissue(r, c2):
                _token_copy(nch, zbuf, 0, xs_hbm, start + r, sem_z).start()
                return c2

            def zwait(r, c2):
                _token_copy(nch, zbuf, 0, xs_hbm, 0, sem_z).wait()
                return c2

            lax.fori_loop(0, num, zissue, 0)
            lax.fori_loop(0, num, zwait, 0)
            return carry

        lax.fori_loop(0, n_exp, per_expert, 0)

        def blk_copy(bi):
            row = pl.multiple_of(bi * blk_rows, blk_rows)
            return pltpu.make_async_copy(zbuf, xs_hbm.at[pl.ds(row, blk_rows)], sem_z)

        def bissue(bi, carry):
            blk_copy(bi).start()
            return carry

        def bwait(bi, carry):
            blk_copy(bi).wait()
            return carry

        lax.fori_loop(nused_ref[0], n_blocks, bissue, 0)
        lax.fori_loop(nused_ref[0], n_blocks, bwait, 0)


def _dispatch(zstart, znum, n_used, dest, h1_tt, nch, n_slots, tm):
    t = h1_tt.shape[0] // nch
    n_exp = zstart.shape[0]
    dest2 = dest.reshape(t // tm, tm * TOP_K)
    return pl.pallas_call(
        functools.partial(_dispatch_kernel, tm, nch, n_exp, n_slots // MOE_BLOCK),
        grid_spec=pltpu.PrefetchScalarGridSpec(
            num_scalar_prefetch=3,
            grid=(t // tm,),
            in_specs=[pl.BlockSpec(memory_space=pl.ANY), pl.BlockSpec(memory_space=pl.ANY)],
            out_specs=pl.BlockSpec(memory_space=pl.ANY),
            scratch_shapes=[pltpu.SMEM((2, tm * TOP_K), I32),
                            pltpu.VMEM((N_TILE_BUF, tm * nch, LANES), F32),
                            pltpu.VMEM((MOE_BLOCK * nch, LANES), F32),
                            pltpu.SemaphoreType.DMA((2,)), pltpu.SemaphoreType.DMA((N_TILE_BUF,)),
                            pltpu.SemaphoreType.DMA((2,)), pltpu.SemaphoreType.DMA],
        ),
        out_shape=jax.ShapeDtypeStruct((n_slots * nch, LANES), F32),
        compiler_params=_cparams(1),
        name="moe_dispatch",
    )(zstart, znum, n_used, dest2, h1_tt)


def _expert_kernel(d_ff, nch, be_ref, nused_ref, x_ref, wgu_ref, bgu_ref, wd_ref, bd_ref, y_ref):
    i = pl.program_id(0)

    @pl.when(i < nused_ref[0])
    def _():
        x = _load_token_tiles(x_ref, MOE_BLOCK, nch).astype(BF16)
        gu = jnp.dot(x, wgu_ref[0], preferred_element_type=F32) + bgu_ref[0]
        gate = jnp.minimum(gu[:, :d_ff], SWIGLU_LIMIT)
        up = jnp.clip(gu[:, d_ff:], -SWIGLU_LIMIT, SWIGLU_LIMIT)
        act = gate * jax.nn.sigmoid(SWIGLU_ALPHA * gate) * (up + 1.0)
        y = jnp.dot(act.astype(BF16), wd_ref[0], preferred_element_type=F32) + bd_ref[0]
        _store_token_tiles(y_ref, y)

    @pl.when(i >= nused_ref[0])
    def _():
        y_ref[...] = jnp.zeros_like(y_ref)


def _experts(block_e, n_used, xs_tt, wgu, bgu, wd, bd):
    n_exp, d, two_ff = wgu.shape
    nch = d // LANES
    n_slots = xs_tt.shape[0] // nch
    d_ff = two_ff // 2
    nb = n_slots // MOE_BLOCK
    live = lambda i, be, nu: jnp.minimum(i, nu[0] - 1)
    return pl.pallas_call(
        functools.partial(_expert_kernel, d_ff, nch),
        grid_spec=pltpu.PrefetchScalarGridSpec(
            num_scalar_prefetch=2,
            grid=(nb,),
            in_specs=[
                pl.BlockSpec((MOE_BLOCK * nch, LANES), lambda i, be, nu: (live(i, be, nu), 0)),
                pl.BlockSpec((1, d, two_ff), lambda i, be, nu: (be[live(i, be, nu)], 0, 0)),
                pl.BlockSpec((1, 1, two_ff), lambda i, be, nu: (be[live(i, be, nu)], 0, 0)),
                pl.BlockSpec((1, d_ff, d), lambda i, be, nu: (be[live(i, be, nu)], 0, 0)),
                pl.BlockSpec((1, 1, d), lambda i, be, nu: (be[live(i, be, nu)], 0, 0)),
            ],
            out_specs=pl.BlockSpec((MOE_BLOCK * nch, LANES), lambda i, be, nu: (i, 0)),
        ),
        out_shape=jax.ShapeDtypeStruct((n_slots * nch, LANES), F32),
        compiler_params=_cparams(1),
        name="moe_experts",
    )(block_e, n_used, xs_tt, wgu, bgu.reshape(n_exp, 1, two_ff), wd, bd.reshape(n_exp, 1, d))


def _combine_kernel(tm, nch, alpha, n_first, split, dest_hbm, h1_ref, wt_ref, g_ref, b_ref, y_hbm,
                    *rest):
    if split:
        oa_ref, ob_ref, dsm, ybuf, sem_idx, sem = rest
    else:
        o_ref, ob16_ref, dsm, ybuf, sem_idx, sem = rest
    i = pl.program_id(0)
    n = pl.num_programs(0)
    slot = lax.rem(i, 2)

    def idx_fetch(step, s):
        return pltpu.make_async_copy(dest_hbm.at[step], dsm.at[s], sem_idx.at[s])

    def gather(s):
        def issue(r, carry):
            for k in range(TOP_K):
                _token_copy(nch, y_hbm, dsm[s, r * TOP_K + k], ybuf.at[s, k], r,
                            sem.at[s]).start(priority=k % 2)
            return carry

        lax.fori_loop(0, tm, issue, 0, unroll=ISSUE_UNROLL)

    @pl.when(i == 0)
    def _():
        first = idx_fetch(0, 0)
        first.start()
        first.wait()
        gather(0)

        @pl.when(n > 1)
        def _():
            idx_fetch(1, 1).start()

    @pl.when(i + 1 < n)
    def _():
        idx_fetch(i + 1, 1 - slot).wait()
        gather(1 - slot)

    @pl.when(i + 2 < n)
    def _():
        idx_fetch(i + 2, slot).start()

    for k in range(TOP_K):
        pltpu.make_async_copy(y_hbm.at[pl.ds(0, tm * nch)], ybuf.at[slot, k], sem.at[slot]).wait()
    wt = wt_ref[...]
    f = wt[:, 0:1] * _load_token_tiles(ybuf.at[slot, 0], tm, nch)
    for k in range(1, TOP_K):
        f = f + wt[:, k:k + 1] * _load_token_tiles(ybuf.at[slot, k], tm, nch)
    h1 = _load_token_tiles(h1_ref, tm, nch)
    out = _layer_norm(alpha * h1 + f, g_ref[...], b_ref[...])
    if split:
        @pl.when(i < n_first)
        def _():
            oa_ref[...] = out

        @pl.when(i >= n_first)
        def _():
            ob_ref[...] = out
    else:
        o_ref[...] = out
        ob16_ref[...] = out.astype(BF16)


def _combine(alpha, dest, h1_tt, wt128, g, b, y_tt, tm, split_rows=None):
    d = g.shape[0]
    nch = d // LANES
    t = h1_tt.shape[0] // nch
    nsteps = t // tm
    dest2 = dest.reshape(nsteps, tm * TOP_K)
    split = split_rows is not None
    row = lambda c: pl.BlockSpec((tm, c), lambda i: (i, 0))
    vec = pl.BlockSpec((1, d), lambda i: (0, 0))
    if split:
        na = split_rows // tm
        out_specs = [pl.BlockSpec((tm, d), lambda i: (jnp.minimum(i, na - 1), 0)),
                     pl.BlockSpec((tm, d), lambda i: (jnp.maximum(i - na, 0), 0))]
        out_shape = [jax.ShapeDtypeStruct((split_rows, d), F32),
                     jax.ShapeDtypeStruct((t - split_rows, d), F32)]
    else:
        na = 0
        out_specs = [row(d), row(d)]
        out_shape = [jax.ShapeDtypeStruct((t, d), F32), jax.ShapeDtypeStruct((t, d), BF16)]
    return pl.pallas_call(
        functools.partial(_combine_kernel, tm, nch, alpha, na, split),
        grid=(nsteps,),
        in_specs=[pl.BlockSpec(memory_space=pl.ANY),
                  pl.BlockSpec((tm * nch, LANES), lambda i: (i, 0)), row(LANES), vec, vec,
                  pl.BlockSpec(memory_space=pl.ANY)],
        out_specs=out_specs,
        out_shape=out_shape,
        scratch_shapes=[pltpu.SMEM((2, tm * TOP_K), I32),
                        pltpu.VMEM((2, TOP_K, tm * nch, LANES), F32),
                        pltpu.SemaphoreType.DMA((2,)), pltpu.SemaphoreType.DMA((2,))],
        compiler_params=_cparams(1),
        name="moe_combine",
    )(dest2, h1_tt, wt128, g.reshape(1, d), b.reshape(1, d), y_tt)


def _route_plan(idx, rank, counts, n_blocks):
    padded = (counts + MOE_BLOCK - 1) // MOE_BLOCK * MOE_BLOCK
    pad_ends = jnp.cumsum(padded)
    pad_starts = pad_ends - padded
    dest = pad_starts[idx] + rank
    block_start = jnp.arange(n_blocks, dtype=I32) * MOE_BLOCK
    n_exp = counts.shape[0]
    block_e = jnp.minimum(jnp.sum(block_start[:, None] >= pad_ends[None, :], axis=1), n_exp - 1).astype(I32)
    n_used = (pad_ends[-1] // MOE_BLOCK).astype(I32).reshape(1)
    return dest.astype(I32), block_e, n_used, (pad_starts + counts).astype(I32), (padded - counts).astype(I32)


def _tile(n, pref):
    t = min(pref, n)
    while n % t:
        t //= 2
    return t


def kernel(x_prompt, x_sample, ln_in_g, ln_in_b, w_in, conv_w, conv_b, filt_w1, filt_b1, filt_freq1, filt_w2, filt_b2, filt_freq2, filt_w3, filt_decay, hy_bias, w_hy_out, w_fn_out, w_o, ln1_g, ln1_b, w_router, b_router, w_gu, b_gu, w_down, b_down, ln2_g, ln2_b):
    bp, lp, d = x_prompt.shape
    bs, ls, _ = x_sample.shape
    depth = w_in.shape[0]
    hy = hy_bias.shape[-1]
    fn = w_fn_out.shape[1]
    n_exp = w_router.shape[-1]
    tp, ts = bp * lp, bs * ls
    t = tp + ts
    alpha = (2 * depth) ** 0.25
    trunks = ((0, tp, bp, lp), (tp, ts, bs, ls))
    n_blocks = -(-(t * TOP_K) // MOE_BLOCK) + n_exp
    n_slots = n_blocks * MOE_BLOCK

    tm_ln = _tile(math.gcd(tp, ts), 1024)
    tm_tok = _tile(math.gcd(lp, ls), 512)
    tm_row = _tile(math.gcd(tp, ts), 256)

    h, hb = _ln_in(x_prompt.reshape(tp, d), x_sample.reshape(ts, d), ln_in_g, ln_in_b, tm_ln)
    out = None
    for l in range(depth):
        x0, vv, z_fn, gates = _proj(hb, w_in[l].astype(BF16), conv_w[l], conv_b[l], hy, fn,
                                    tp, lp, ls, tm_tok)
        filt_args = (filt_w1[l], filt_b1[l], filt_freq1[l], filt_w2[l], filt_b2[l], filt_freq2[l],
                     filt_w3[l], filt_decay[l])
        conv = jnp.concatenate(
            [_hyena_conv(vv[o:o + n], b, s, filt_args, 8) for (o, n, b, s) in trunks], axis=0)
        yfn = jnp.concatenate(
            [_fnet_mix(z_fn[o:o + n], b, s, 8) for (o, n, b, s) in trunks], axis=0)
        h1, idx128, wt128 = _merge(alpha, conv, vv, x0, yfn, gates, h, hy_bias[l],
                                   w_hy_out[l].astype(BF16), w_fn_out[l].astype(BF16),
                                   w_o[l].astype(BF16), ln1_g[l], ln1_b[l], w_router[l], b_router[l],
                                   tm_tok)
        rank128, cnt128 = _ranks(idx128, tm_tok)
        dest, block_e, n_used, zstart, znum = _route_plan(
            idx128[:, :TOP_K], rank128[:, :TOP_K], cnt128[0, :n_exp], n_blocks)
        xs = _dispatch(zstart, znum, n_used, dest, h1, d // LANES, n_slots, tm_row)
        y = _experts(block_e, n_used, xs, w_gu[l].astype(BF16), b_gu[l], w_down[l].astype(BF16),
                     b_down[l])
        if l + 1 < depth:
            h, hb = _combine(alpha, dest, h1, wt128, ln2_g[l], ln2_b[l], y, tm_row)
        else:
            out = _combine(alpha, dest, h1, wt128, ln2_g[l], ln2_b[l], y, tm_row, split_rows=tp)
    return (out[0].reshape(bp, lp, d), out[1].reshape(bs, ls, d))
```

```python
import functools
import math

import jax
import jax.numpy as jnp
from jax import lax
from jax.experimental import pallas as pl
from jax.experimental.pallas import tpu as pltpu

F32 = jnp.float32
BF16 = jnp.bfloat16
I32 = jnp.int32

TOP_K = 4
FN_GROUPS = 4
MOE_BLOCK = 512
FILTER_BANDS = 16
MOD_SHIFT = 0.05
SWIGLU_LIMIT = 7.0
SWIGLU_ALPHA = 1.702
LN_EPS = 1e-5
LANES = 128
DFT_RADIX = 128
FNET_MINOR = 64
VMEM_LIMIT = 48 * 1024 * 1024
VMEM_LIMIT_EXPERTS = 56 * 1024 * 1024


def _cparams(ndim, vmem=VMEM_LIMIT):
    return pltpu.CompilerParams(dimension_semantics=("arbitrary",) * ndim,
                                vmem_limit_bytes=vmem)


def _store_token_tiles(ref, x):
    rows, d = x.shape
    nch = d // LANES
    for c in range(nch):
        ref[pl.ds(c, rows, stride=nch), :] = x[:, c * LANES:(c + 1) * LANES]


def _load_token_tiles(ref, rows, nch):
    return jnp.concatenate([ref[pl.ds(c, rows, stride=nch), :] for c in range(nch)], axis=1)


def _layer_norm(x, g, b):
    mu = jnp.mean(x, axis=-1, keepdims=True)
    xc = x - mu
    var = jnp.mean(xc * xc, axis=-1, keepdims=True)
    return xc * lax.rsqrt(var + LN_EPS) * g + b


def _ln_in_kernel(n_first, xa_ref, xb_ref, g_ref, b_ref, h_ref, hb_ref):
    i = pl.program_id(0)

    def emit(x_ref):
        y = _layer_norm(x_ref[...], g_ref[...], b_ref[...])
        h_ref[...] = y
        hb_ref[...] = y.astype(BF16)

    @pl.when(i < n_first)
    def _():
        emit(xa_ref)

    @pl.when(i >= n_first)
    def _():
        emit(xb_ref)


def _ln_in(xa, xb, g, b, tm):
    ta, d = xa.shape
    tb = xb.shape[0]
    na, nb = ta // tm, tb // tm
    t = ta + tb
    return pl.pallas_call(
        functools.partial(_ln_in_kernel, na),
        grid=(na + nb,),
        in_specs=[
            pl.BlockSpec((tm, d), lambda i: (jnp.minimum(i, na - 1), 0)),
            pl.BlockSpec((tm, d), lambda i: (jnp.maximum(i - na, 0), 0)),
            pl.BlockSpec((1, d), lambda i: (0, 0)),
            pl.BlockSpec((1, d), lambda i: (0, 0)),
        ],
        out_specs=[pl.BlockSpec((tm, d), lambda i: (i, 0)),
                   pl.BlockSpec((tm, d), lambda i: (i, 0))],
        out_shape=[jax.ShapeDtypeStruct((t, d), F32), jax.ShapeDtypeStruct((t, d), BF16)],
        compiler_params=_cparams(1),
        name="ln_in",
    )(xa, xb, g.reshape(1, d), b.reshape(1, d))


HALO = 16


def _proj_kernel(tm, t_first, l_first, l_second, hy, fn, x_ref, xp_ref, xn_ref, w_ref, cw_ref, cb_ref,
                 x0_ref, vva_ref, vvb_ref, zfa_ref, zfb_ref, gate_ref):
    i = pl.program_id(0)
    r0 = i * tm
    seq = jnp.where(r0 < t_first, l_first, l_second)
    is_first = lax.rem(r0, seq) == 0
    is_last = lax.rem(r0 + tm, seq) == 0
    x = x_ref[...]
    x_ext = jnp.concatenate([xp_ref[...], x, xn_ref[...]], axis=0)
    row = lax.broadcasted_iota(I32, (tm, 1), 0)
    kill_prev = jnp.logical_and(row == 0, is_first)
    kill_next = jnp.logical_and(row == tm - 1, is_last)

    def conv_chunk(c):
        lo, hi = c * hy, (c + 1) * hy
        z = jnp.dot(x_ext, w_ref[:, lo:hi], preferred_element_type=F32)
        um1 = jnp.where(kill_prev, 0.0, z[HALO - 1:HALO - 1 + tm])
        up1 = jnp.where(kill_next, 0.0, z[HALO + 1:HALO + 1 + tm])
        return (um1 * cw_ref[0:1, lo:hi] + z[HALO:HALO + tm] * cw_ref[1:2, lo:hi]
                + up1 * cw_ref[2:3, lo:hi] + cb_ref[:, lo:hi])

    x0_ref[...] = conv_chunk(0).astype(BF16)
    vv = (conv_chunk(2) * conv_chunk(1)).astype(BF16)
    zfn = jnp.dot(x, w_ref[:, 3 * hy:3 * hy + fn], preferred_element_type=F32).astype(BF16)

    @pl.when(r0 < t_first)
    def _():
        vva_ref[...] = vv
        zfa_ref[...] = zfn

    @pl.when(r0 >= t_first)
    def _():
        vvb_ref[...] = vv
        zfb_ref[...] = zfn

    g = jnp.dot(x, w_ref[:, 3 * hy + fn:], preferred_element_type=F32)
    gate_ref[...] = jax.nn.sigmoid(g).astype(BF16)


def _proj(hb, w_in, conv_w, conv_b, hy, fn, t_first, l_first, l_second, tm):
    t, d = hb.shape
    n = w_in.shape[1]
    ng = n - 3 * hy - fn
    nh = tm // HALO
    last = t // HALO - 1
    row = lambda c: pl.BlockSpec((tm, c), lambda i: (i, 0))
    na = t_first // tm
    row_a = lambda c: pl.BlockSpec((tm, c), lambda i: (jnp.minimum(i, na - 1), 0))
    row_b = lambda c: pl.BlockSpec((tm, c), lambda i: (jnp.maximum(i - na, 0), 0))
    bf = lambda r, c: jax.ShapeDtypeStruct((r, c), BF16)
    return pl.pallas_call(
        functools.partial(_proj_kernel, tm, t_first, l_first, l_second, hy, fn),
        grid=(t // tm,),
        in_specs=[
            row(d),
            pl.BlockSpec((HALO, d), lambda i: (jnp.maximum(i * nh - 1, 0), 0)),
            pl.BlockSpec((HALO, d), lambda i: (jnp.minimum((i + 1) * nh, last), 0)),
            pl.BlockSpec((d, n), lambda i: (0, 0)),
            pl.BlockSpec((3, 3 * hy), lambda i: (0, 0)),
            pl.BlockSpec((1, 3 * hy), lambda i: (0, 0)),
        ],
        out_specs=[row(hy), row_a(hy), row_b(hy), row_a(fn), row_b(fn), row(ng)],
        out_shape=[bf(t, hy), bf(t_first, hy), bf(t - t_first, hy),
                   bf(t_first, fn), bf(t - t_first, fn), bf(t, ng)],
        compiler_params=_cparams(1),
        name="proj_hyena_pre",
    )(hb, hb, hb, w_in, conv_w, conv_b.reshape(1, 3 * hy))


def _filter_kernel(tr, seq, w1_ref, b1_ref, f1_ref, w2_ref, b2_ref, f2_ref, w3_ref, dec_ref, o_ref):
    i = pl.program_id(0)
    n = i * tr + lax.broadcasted_iota(I32, (tr, 1), 0)
    m = jnp.where(n < seq, n, 2 * seq - n).astype(F32)
    t = m * (1.0 / (seq - 1))
    ang = (2.0 * math.pi / seq) * m
    band = lax.broadcasted_iota(I32, (1, FILTER_BANDS), 1).astype(F32)
    freqs = 1e-4 + band * ((FILTER_BANDS - 1 - 1e-4) / (FILTER_BANDS - 1))
    fa = ang * freqs
    hi = lax.Precision.HIGHEST
    pre = (t * w1_ref[0:1, :]
           + jnp.dot(jnp.cos(fa), w1_ref[1:1 + FILTER_BANDS, :], precision=hi, preferred_element_type=F32)
           - jnp.dot(jnp.sin(fa), w1_ref[1 + FILTER_BANDS:, :], precision=hi, preferred_element_type=F32))
    h = jnp.sin(f1_ref[...] * (pre + b1_ref[...]))
    h = jnp.sin(f2_ref[...] * (jnp.dot(h, w2_ref[...], precision=hi, preferred_element_type=F32)
                               + b2_ref[...]))
    h = jnp.dot(h, w3_ref[...], precision=hi, preferred_element_type=F32)
    h = h * (jnp.exp(-t * jnp.abs(dec_ref[...])) + MOD_SHIFT)
    o_ref[...] = jnp.where(n == seq, 0.0, h)


def _filter(seq, w1, b1, f1, w2, b2, f2, w3, dec, tr):
    hid = w1.shape[1]
    hy = w3.shape[1] // 2
    nfwd = seq // tr
    side = lambda i: (0, jnp.where(i >= nfwd, 1, 0))
    full = lambda i: (0, 0)
    return pl.pallas_call(
        functools.partial(_filter_kernel, tr, seq),
        grid=(2 * seq // tr,),
        in_specs=[
            pl.BlockSpec(w1.shape, full), pl.BlockSpec((1, hid), full), pl.BlockSpec((1, hid), full),
            pl.BlockSpec(w2.shape, full), pl.BlockSpec((1, hid), full), pl.BlockSpec((1, hid), full),
            pl.BlockSpec((hid, hy), side), pl.BlockSpec((1, hy), side),
        ],
        out_specs=pl.BlockSpec((tr, hy), lambda i: (i, 0)),
        out_shape=jax.ShapeDtypeStruct((2 * seq, hy), F32),
        compiler_params=_cparams(1),
        name="hyena_filter",
    )(w1, b1.reshape(1, hid), f1.reshape(1, hid), w2, b2.reshape(1, hid), f2.reshape(1, hid),
      w3, dec.reshape(1, 2 * hy))


def _blm_kernel(tj, shared, m_ref, x_ref, o_ref):
    for jj in range(tj):
        m = m_ref[0 if shared else jj]
        o_ref[0, jj] = jnp.dot(m, x_ref[0, jj], preferred_element_type=F32).astype(o_ref.dtype)


def _blm(m, x, tj, out_dtype=BF16):
    g, j, k, c = x.shape
    jm, mr, _ = m.shape
    shared = jm == 1
    tj = min(tj, j)
    m_spec = (pl.BlockSpec((1, mr, k), lambda jb, gb: (0, 0, 0)) if shared
              else pl.BlockSpec((tj, mr, k), lambda jb, gb: (jb, 0, 0)))
    return pl.pallas_call(
        functools.partial(_blm_kernel, tj, shared),
        grid=(j // tj, g),
        in_specs=[m_spec, pl.BlockSpec((1, tj, k, c), lambda jb, gb: (gb, jb, 0, 0))],
        out_specs=pl.BlockSpec((1, tj, mr, c), lambda jb, gb: (gb, jb, 0, 0)),
        out_shape=jax.ShapeDtypeStruct((g, j, mr, c), out_dtype),
        compiler_params=_cparams(2),
        name="dft_stage",
    )(m, x)


def _fnet_s1_kernel(tj, fn, m_ref, r_ref, x_ref, o_ref):
    for jj in range(tj):
        z = jnp.dot(x_ref[0, jj], r_ref[...], preferred_element_type=F32)
        zs = jnp.concatenate([z[:, :fn], z[:, fn:]], axis=0).astype(BF16)
        o_ref[0, jj] = jnp.dot(m_ref[jj], zs, preferred_element_type=F32).astype(o_ref.dtype)


def _fnet_s1(m, r, x, tj):
    g, j, k, fn = x.shape
    _, mr, k2 = m.shape
    tj = min(tj, j)
    return pl.pallas_call(
        functools.partial(_fnet_s1_kernel, tj, fn),
        grid=(j // tj, g),
        in_specs=[pl.BlockSpec((tj, mr, k2), lambda jb, gb: (jb, 0, 0)),
                  pl.BlockSpec(r.shape, lambda jb, gb: (0, 0)),
                  pl.BlockSpec((1, tj, k, fn), lambda jb, gb: (gb, jb, 0, 0))],
        out_specs=pl.BlockSpec((1, tj, mr, fn), lambda jb, gb: (gb, jb, 0, 0)),
        out_shape=jax.ShapeDtypeStruct((g, j, mr, fn), BF16),
        compiler_params=_cparams(2),
        name="fnet_stage1",
    )(m, r, x)


def _hyena_mid_kernel(tk, n2, f_ref, kf_ref, gi_ref, x_ref, o_ref):
    for kk in range(tk):
        a = jnp.dot(f_ref[...], x_ref[0, kk], preferred_element_type=F32)
        ar, ai = a[:n2], a[n2:]
        kr = kf_ref[kk, :n2].astype(F32)
        ki = kf_ref[kk, n2:].astype(F32)
        prod = jnp.concatenate([ar * kr - ai * ki, ar * ki + ai * kr], axis=0).astype(BF16)
        o_ref[0, kk] = jnp.dot(gi_ref[kk], prod, preferred_element_type=F32).astype(o_ref.dtype)


def _hyena_mid(f2s, kf, ginv, x, tk):
    g, n1, r, c = x.shape
    n2 = r // 2
    tk = min(tk, n1)
    return pl.pallas_call(
        functools.partial(_hyena_mid_kernel, tk, n2),
        grid=(n1 // tk, g),
        in_specs=[pl.BlockSpec((r, r), lambda kb, gb: (0, 0)),
                  pl.BlockSpec((tk, r, c), lambda kb, gb: (kb, 0, 0)),
                  pl.BlockSpec((tk, r, r), lambda kb, gb: (kb, 0, 0)),
                  pl.BlockSpec((1, tk, r, c), lambda kb, gb: (gb, kb, 0, 0))],
        out_specs=pl.BlockSpec((1, tk, r, c), lambda kb, gb: (gb, kb, 0, 0)),
        out_shape=jax.ShapeDtypeStruct(x.shape, BF16),
        compiler_params=_cparams(2),
        name="hyena_mid",
    )(f2s, kf, ginv, x)


def _cis(num, den, sign):
    ang = (2.0 * math.pi / den) * lax.rem(num, den).astype(F32)
    return jnp.cos(ang), sign * jnp.sin(ang)


def _stack_complex(mr, mi):
    top = jnp.concatenate([mr, -mi], axis=-1)
    bot = jnp.concatenate([mi, mr], axis=-1)
    return jnp.concatenate([top, bot], axis=-2)


def _iota(shape, axis):
    return lax.broadcasted_iota(I32, shape, axis)


def _hyena_mats(n, n1, n2):
    n1h = n1 // 2
    sh = (n2, n1, n1)
    gr, gi = _cis(_iota(sh, 1) * (n2 * _iota(sh, 2) + _iota(sh, 0)), n, -1.0)
    g_data = _stack_complex(gr[:, :, :n1h], gi[:, :, :n1h]).astype(BF16)
    g_filt = jnp.concatenate([gr, gi], axis=1).astype(BF16)
    sh = (n2, n2)
    fr, fi = _cis(_iota(sh, 0) * _iota(sh, 1), n2, -1.0)
    f2s = _stack_complex(fr, fi).astype(BF16)
    sh = (n1, n2, n2)
    ir, ii = _cis(_iota(sh, 1) * (_iota(sh, 0) + n1 * _iota(sh, 2)), n, 1.0)
    ginv = _stack_complex(ir, ii).astype(BF16)
    sh = (n1h, n1)
    br, bi = _cis(_iota(sh, 0) * _iota(sh, 1), n1, 1.0)
    f1inv = (_stack_complex(br, bi) * (1.0 / n)).astype(BF16)[None]
    return g_data, g_filt, f2s, ginv, f1inv


def _fnet_mats(seq, na, nb, fn):
    gd = fn // FN_GROUPS
    sh = (nb, na, na)
    gr, gi = _cis(_iota(sh, 1) * (nb * _iota(sh, 2) + _iota(sh, 0)), seq, -1.0)
    g1 = _stack_complex(gr, gi).astype(BF16)
    sh = (nb, nb)
    fr, fi = _cis(_iota(sh, 0) * _iota(sh, 1), nb, -1.0)
    scale = 1.0 / math.sqrt(seq * gd)
    f2re = (jnp.concatenate([fr, -fi], axis=-1) * scale).astype(BF16)[None]
    sh = (fn, fn)
    same = (_iota(sh, 0) // gd) == (_iota(sh, 1) // gd)
    cr, ci = _cis(_iota(sh, 0) * _iota(sh, 1), gd, -1.0)
    rmat = jnp.concatenate([jnp.where(same, cr, 0.0), jnp.where(same, ci, 0.0)], axis=1).astype(BF16)
    return g1, f2re, rmat


def _hyena_conv(vv, batch, seq, filt_args, tj):
    c = vv.shape[1]
    n = 2 * seq
    n1 = DFT_RADIX
    n2 = n // n1
    n1h = n1 // 2
    bp = batch // 2
    g_data, g_filt, f2s, ginv, f1inv = _hyena_mats(n, n1, n2)
    kt = _filter(seq, *filt_args, tr=min(512, seq))
    kt = kt.reshape(1, n1, n2, c).transpose(0, 2, 1, 3).astype(BF16)
    ks = _blm(g_filt, kt, tj)
    ks = ks.reshape(1, n2, 2, n1, c).transpose(0, 3, 2, 1, 4).reshape(1, n1, 2 * n2, c)
    kf = _blm(f2s[None], ks, tj, out_dtype=F32)[0]
    x = vv.reshape(bp, 2, n1h, n2, c).transpose(0, 3, 1, 2, 4).reshape(bp, n2, n1, c)
    a = _blm(g_data, x, tj)
    a = a.reshape(bp, n2, 2, n1, c).transpose(0, 3, 2, 1, 4).reshape(bp, n1, 2 * n2, c)
    z = _hyena_mid(f2s, kf, ginv, a, 4)
    z = z.reshape(bp, n1, 2, n2, c).transpose(0, 3, 2, 1, 4).reshape(bp, n2, 2 * n1, c)
    y = _blm(f1inv, z, tj)
    y = y.reshape(bp, n2, 2, n1h, c).transpose(0, 2, 3, 1, 4)
    return y.reshape(batch * seq, c)


def _fnet_mix(u, batch, seq, tj):
    fn = u.shape[1]
    nb = FNET_MINOR
    na = seq // nb
    g1, f2re, rmat = _fnet_mats(seq, na, nb, fn)
    x = u.reshape(batch, na, nb, fn).transpose(0, 2, 1, 3)
    a = _fnet_s1(g1, rmat, x, tj)
    a = a.reshape(batch, nb, 2, na, fn).transpose(0, 3, 2, 1, 4).reshape(batch, na, 2 * nb, fn)
    y = _blm(f2re, a, tj)
    return y.transpose(0, 2, 1, 3).reshape(batch * seq, fn)


def _merge_kernel(alpha, d, n_exp, n_first, conva_ref, convb_ref, vva_ref, vvb_ref, yfna_ref, yfnb_ref,
                  x0_ref, gate_ref, h_ref, bias_ref,
                  why_ref, wfn_ref, wo_ref, g_ref, b_ref, wr_ref, br_ref,
                  h1_ref, idx_ref, wt_ref):
    in_first = pl.program_id(0) < n_first
    pick = lambda a_ref, b_ref: jnp.where(in_first, a_ref[...], b_ref[...])
    vv = pick(vva_ref, vvb_ref).astype(F32)
    y_hy = x0_ref[...].astype(F32) * (pick(conva_ref, convb_ref).astype(F32) + vv * bias_ref[...])
    a = jnp.dot(y_hy.astype(BF16), why_ref[...], preferred_element_type=F32)
    bfn = jnp.dot(pick(yfna_ref, yfnb_ref), wfn_ref[...], preferred_element_type=F32)
    gates = gate_ref[...].astype(F32)
    merged = gates[:, :d] * a + gates[:, d:] * bfn
    m = jnp.dot(merged.astype(BF16), wo_ref[...], preferred_element_type=F32)
    h1 = _layer_norm(alpha * h_ref[...] + m, g_ref[...], b_ref[...])
    _store_token_tiles(h1_ref, h1)
    logits = jnp.dot(h1.astype(BF16), wr_ref[...], preferred_element_type=F32) + br_ref[...]
    lane = lax.broadcasted_iota(I32, logits.shape, 1)
    lane_f = lane.astype(F32)
    neg = jnp.float32(-jnp.inf)
    logits = jnp.where(lane < n_exp, logits, neg)
    vals, idxs = [], []
    for _ in range(TOP_K):
        mx = jnp.max(logits, axis=-1, keepdims=True)
        ix = jnp.min(jnp.where(logits == mx, lane_f, float(LANES)), axis=-1, keepdims=True).astype(I32)
        vals.append(mx)
        idxs.append(ix)
        logits = jnp.where(lane == ix, neg, logits)
    exps = [jnp.exp(v - vals[0]) for v in vals]
    den = exps[0]
    for e in exps[1:]:
        den = den + e
    idx_out = jnp.zeros(lane.shape, I32)
    wt_out = jnp.zeros(lane.shape, F32)
    for k in range(TOP_K):
        idx_out = jnp.where(lane == k, idxs[k], idx_out)
        wt_out = jnp.where(lane == k, exps[k] / den, wt_out)
    idx_ref[...] = idx_out
    wt_ref[...] = wt_out


def _merge(alpha, convs, vvs, yfns, x0, gates, h, bias, why, wfn, wo, g, b, wr, br, tm):
    t, d = h.shape
    hy = x0.shape[1]
    fn = yfns[0].shape[1]
    n_exp = wr.shape[1]
    na = convs[0].shape[0] // tm
    wr_p = jnp.zeros((d, LANES), BF16).at[:, :n_exp].set(wr.astype(BF16))
    br_p = jnp.zeros((1, LANES), F32).at[0, :n_exp].set(br)
    row = lambda c: pl.BlockSpec((tm, c), lambda i: (i, 0))
    row_a = lambda c: pl.BlockSpec((tm, c), lambda i: (jnp.minimum(i, na - 1), 0))
    row_b = lambda c: pl.BlockSpec((tm, c), lambda i: (jnp.maximum(i - na, 0), 0))
    full = lambda a: pl.BlockSpec(a.shape, lambda i: (0,) * a.ndim)
    args = (convs[0], convs[1], vvs[0], vvs[1], yfns[0], yfns[1], x0, gates, h,
            bias.reshape(1, hy), why, wfn, wo, g.reshape(1, d), b.reshape(1, d), wr_p, br_p)
    in_specs = ([row_a(hy), row_b(hy), row_a(hy), row_b(hy), row_a(fn), row_b(fn),
                 row(hy), row(2 * d), row(d)] + [full(a) for a in args[9:]])
    return pl.pallas_call(
        functools.partial(_merge_kernel, alpha, d, n_exp, na),
        grid=(t // tm,),
        in_specs=in_specs,
        out_specs=[pl.BlockSpec((tm * (d // LANES), LANES), lambda i: (i, 0)), row(LANES), row(LANES)],
        out_shape=[jax.ShapeDtypeStruct((t * (d // LANES), LANES), F32),
                   jax.ShapeDtypeStruct((t, LANES), I32), jax.ShapeDtypeStruct((t, LANES), F32)],
        compiler_params=_cparams(1),
        name="merge_route",
    )(*args)


def _rank_kernel(tm, idx_ref, rank_ref, cnt_ref, carry_ref):
    i = pl.program_id(0)

    @pl.when(i == 0)
    def _():
        carry_ref[...] = jnp.zeros_like(carry_ref)

    idx = idx_ref[...]
    lane = lax.broadcasted_iota(I32, idx.shape, 1)
    cols = [idx[:, k:k + 1] for k in range(TOP_K)]
    onehot = jnp.zeros(idx.shape, F32)
    for ck in cols:
        onehot = onehot + (lane == ck).astype(F32)
    r = lax.broadcasted_iota(I32, (tm, tm), 0)
    c = lax.broadcasted_iota(I32, (tm, tm), 1)
    tri = (c < r).astype(BF16)
    before = jnp.dot(tri, onehot.astype(BF16), preferred_element_type=F32) + carry_ref[...]
    out = jnp.zeros(idx.shape, F32)
    for k, ck in enumerate(cols):
        rk = jnp.sum(jnp.where(lane == ck, before, 0.0), axis=-1, keepdims=True)
        out = jnp.where(lane == k, rk, out)
    rank_ref[...] = out.astype(I32)
    total = carry_ref[...] + jnp.sum(onehot, axis=0, keepdims=True)
    carry_ref[...] = total
    cnt_ref[...] = total.astype(I32)


def _ranks(idx128, tm):
    t = idx128.shape[0]
    return pl.pallas_call(
        functools.partial(_rank_kernel, tm),
        grid=(t // tm,),
        in_specs=[pl.BlockSpec((tm, LANES), lambda i: (i, 0))],
        out_specs=[pl.BlockSpec((tm, LANES), lambda i: (i, 0)),
                   pl.BlockSpec((1, LANES), lambda i: (0, 0))],
        out_shape=[jax.ShapeDtypeStruct((t, LANES), I32), jax.ShapeDtypeStruct((1, LANES), I32)],
        scratch_shapes=[pltpu.VMEM((1, LANES), F32)],
        compiler_params=_cparams(1),
        name="expert_rank",
    )(idx128)


def _token_copy(nch, src_ref, src_tok, dst_ref, dst_tok, sem):
    rows = lambda tok: pl.ds(tok * nch if isinstance(tok, int) else pl.multiple_of(tok * nch, nch), nch)
    return pltpu.make_async_copy(src_ref.at[rows(src_tok)], dst_ref.at[rows(dst_tok)], sem)


N_TILE_BUF = 3


def _dispatch_kernel(tm, nch, n_exp, n_blocks, zstart_ref, znum_ref, nused_ref, dest_hbm, h_hbm, xs_hbm,
                     dsm, hbuf, zbuf, sem_idx, sem_tile, sem_row, sem_z):
    i = pl.program_id(0)
    n = pl.num_programs(0)
    slot = lax.rem(i, 2)
    tb = lax.rem(i, N_TILE_BUF)
    tile_rows = tm * nch
    blk_rows = MOE_BLOCK * nch

    def idx_fetch(step, s):
        return pltpu.make_async_copy(dest_hbm.at[step], dsm.at[s], sem_idx.at[s])

    def tile_load(step, b):
        row = pl.multiple_of(step * tile_rows, tile_rows)
        return pltpu.make_async_copy(h_hbm.at[pl.ds(row, tile_rows)], hbuf.at[b], sem_tile.at[b])

    def rows_done(s):
        return pltpu.make_async_copy(hbuf.at[0], xs_hbm.at[pl.ds(0, tile_rows)], sem_row.at[s])

    @pl.when(i == 0)
    def _():
        idx_fetch(0, 0).start()
        tile_load(0, 0).start()

    idx_fetch(i, slot).wait()
    tile_load(i, tb).wait()

    @pl.when(i + 1 < n)
    def _():
        idx_fetch(i + 1, 1 - slot).start()
        tile_load(i + 1, lax.rem(i + 1, N_TILE_BUF)).start()

    for r in range(tm):
        for k in range(TOP_K):
            _token_copy(nch, hbuf.at[tb], r, xs_hbm, dsm[slot, r * TOP_K + k],
                        sem_row.at[slot]).start(priority=k % 2)

    @pl.when(i > 0)
    def _():
        for _ in range(TOP_K):
            rows_done(1 - slot).wait()

    @pl.when(i == n - 1)
    def _():
        for _ in range(TOP_K):
            rows_done(slot).wait()
        zbuf[...] = jnp.zeros_like(zbuf)

        def per_expert(e, carry):
            start = zstart_ref[e]
            num = znum_ref[e]

            def zissue(r, c2):
                _token_copy(nch, zbuf, 0, xs_hbm, start + r, sem_z).start()
                return c2

            def zwait(r, c2):
                _token_copy(nch, zbuf, 0, xs_hbm, 0, sem_z).wait()
                return c2

            lax.fori_loop(0, num, zissue, 0)
            lax.fori_loop(0, num, zwait, 0)
            return carry

        lax.fori_loop(0, n_exp, per_expert, 0)

        def blk_copy(bi):
            row = pl.multiple_of(bi * blk_rows, blk_rows)
            return pltpu.make_async_copy(zbuf, xs_hbm.at[pl.ds(row, blk_rows)], sem_z)

        def bissue(bi, carry):
            blk_copy(bi).start()
            return carry

        def bwait(bi, carry):
            blk_copy(bi).wait()
            return carry

        lax.fori_loop(nused_ref[0], n_blocks, bissue, 0)
        lax.fori_loop(nused_ref[0], n_blocks, bwait, 0)


def _dispatch(zstart, znum, n_used, dest, h1_tt, nch, n_slots, tm):
    t = h1_tt.shape[0] // nch
    n_exp = zstart.shape[0]
    dest2 = dest.reshape(t // tm, tm * TOP_K)
    return pl.pallas_call(
        functools.partial(_dispatch_kernel, tm, nch, n_exp, n_slots // MOE_BLOCK),
        grid_spec=pltpu.PrefetchScalarGridSpec(
            num_scalar_prefetch=3,
            grid=(t // tm,),
            in_specs=[pl.BlockSpec(memory_space=pl.ANY), pl.BlockSpec(memory_space=pl.ANY)],
            out_specs=pl.BlockSpec(memory_space=pl.ANY),
            scratch_shapes=[pltpu.SMEM((2, tm * TOP_K), I32),
                            pltpu.VMEM((N_TILE_BUF, tm * nch, LANES), F32),
                            pltpu.VMEM((MOE_BLOCK * nch, LANES), F32),
                            pltpu.SemaphoreType.DMA((2,)), pltpu.SemaphoreType.DMA((N_TILE_BUF,)),
                            pltpu.SemaphoreType.DMA((2,)), pltpu.SemaphoreType.DMA],
        ),
        out_shape=jax.ShapeDtypeStruct((n_slots * nch, LANES), F32),
        compiler_params=_cparams(1),
        name="moe_dispatch",
    )(zstart, znum, n_used, dest2, h1_tt)


def _expert_kernel(d_ff, nch, be_ref, nused_ref, x_ref, wgu_ref, bgu_ref, wd_ref, bd_ref, y_ref,
                   wgu_bf, wd_bf):
    i = pl.program_id(0)
    live = i < nused_ref[0]
    prev = be_ref[jnp.maximum(i - 1, 0)]

    @pl.when(jnp.logical_and(live, jnp.logical_or(i == 0, be_ref[i] != prev)))
    def _():
        wgu_bf[...] = wgu_ref[0].astype(BF16)
        wd_bf[...] = wd_ref[0].astype(BF16)

    @pl.when(live)
    def _():
        x = _load_token_tiles(x_ref, MOE_BLOCK, nch).astype(BF16)
        gu = jnp.dot(x, wgu_bf[...], preferred_element_type=F32) + bgu_ref[0]
        gate = jnp.minimum(gu[:, :d_ff], SWIGLU_LIMIT)
        up = jnp.clip(gu[:, d_ff:], -SWIGLU_LIMIT, SWIGLU_LIMIT)
        act = gate * jax.nn.sigmoid(SWIGLU_ALPHA * gate) * (up + 1.0)
        y = jnp.dot(act.astype(BF16), wd_bf[...], preferred_element_type=F32) + bd_ref[0]
        _store_token_tiles(y_ref, y)

    @pl.when(i >= nused_ref[0])
    def _():
        y_ref[...] = jnp.zeros_like(y_ref)


def _experts(block_e, n_used, xs_tt, layer, wgu_all, bgu, wd_all, bd):
    depth, n_exp, d, two_ff = wgu_all.shape
    nch = d // LANES
    n_slots = xs_tt.shape[0] // nch
    d_ff = two_ff // 2
    nb = n_slots // MOE_BLOCK
    live = lambda i, be, nu: jnp.minimum(i, nu[0] - 1)
    wgu = wgu_all.reshape(depth * n_exp, d, two_ff)
    wd = wd_all.reshape(depth * n_exp, d_ff, d)
    w_row = lambda i, be, nu: (layer * n_exp + be[live(i, be, nu)], 0, 0)
    return pl.pallas_call(
        functools.partial(_expert_kernel, d_ff, nch),
        grid_spec=pltpu.PrefetchScalarGridSpec(
            num_scalar_prefetch=2,
            grid=(nb,),
            in_specs=[
                pl.BlockSpec((MOE_BLOCK * nch, LANES), lambda i, be, nu: (live(i, be, nu), 0)),
                pl.BlockSpec((1, d, two_ff), w_row),
                pl.BlockSpec((1, 1, two_ff), lambda i, be, nu: (be[live(i, be, nu)], 0, 0)),
                pl.BlockSpec((1, d_ff, d), w_row),
                pl.BlockSpec((1, 1, d), lambda i, be, nu: (be[live(i, be, nu)], 0, 0)),
            ],
            out_specs=pl.BlockSpec((MOE_BLOCK * nch, LANES), lambda i, be, nu: (i, 0)),
            scratch_shapes=[pltpu.VMEM((d, two_ff), BF16), pltpu.VMEM((d_ff, d), BF16)],
        ),
        out_shape=jax.ShapeDtypeStruct((n_slots * nch, LANES), F32),
        compiler_params=_cparams(1, VMEM_LIMIT_EXPERTS),
        name="moe_experts",
    )(block_e, n_used, xs_tt, wgu, bgu.reshape(n_exp, 1, two_ff), wd, bd.reshape(n_exp, 1, d))


def _combine_kernel(tm, nch, alpha, n_first, split, dest_hbm, h1_ref, wt_ref, g_ref, b_ref, y_hbm,
                    *rest):
    if split:
        oa_ref, ob_ref, dsm, ybuf, sem_idx, sem = rest
    else:
        o_ref, ob16_ref, dsm, ybuf, sem_idx, sem = rest
    i = pl.program_id(0)
    n = pl.num_programs(0)
    slot = lax.rem(i, 2)

    def idx_fetch(step, s):
        return pltpu.make_async_copy(dest_hbm.at[step], dsm.at[s], sem_idx.at[s])

    def gather(s):
        for r in range(tm):
            for k in range(TOP_K):
                _token_copy(nch, y_hbm, dsm[s, r * TOP_K + k], ybuf.at[s, k], r,
                            sem.at[s]).start(priority=k % 2)

    @pl.when(i == 0)
    def _():
        first = idx_fetch(0, 0)
        first.start()
        first.wait()
        gather(0)

        @pl.when(n > 1)
        def _():
            idx_fetch(1, 1).start()

    @pl.when(i + 1 < n)
    def _():
        idx_fetch(i + 1, 1 - slot).wait()
        gather(1 - slot)

    @pl.when(i + 2 < n)
    def _():
        idx_fetch(i + 2, slot).start()

    for k in range(TOP_K):
        pltpu.make_async_copy(y_hbm.at[pl.ds(0, tm * nch)], ybuf.at[slot, k], sem.at[slot]).wait()
    wt = wt_ref[...]
    f = wt[:, 0:1] * _load_token_tiles(ybuf.at[slot, 0], tm, nch)
    for k in range(1, TOP_K):
        f = f + wt[:, k:k + 1] * _load_token_tiles(ybuf.at[slot, k], tm, nch)
    h1 = _load_token_tiles(h1_ref, tm, nch)
    out = _layer_norm(alpha * h1 + f, g_ref[...], b_ref[...])
    if split:
        @pl.when(i < n_first)
        def _():
            oa_ref[...] = out

        @pl.when(i >= n_first)
        def _():
            ob_ref[...] = out
    else:
        o_ref[...] = out
        ob16_ref[...] = out.astype(BF16)


def _combine(alpha, dest, h1_tt, wt128, g, b, y_tt, tm, split_rows=None):
    d = g.shape[0]
    nch = d // LANES
    t = h1_tt.shape[0] // nch
    nsteps = t // tm
    dest2 = dest.reshape(nsteps, tm * TOP_K)
    split = split_rows is not None
    row = lambda c: pl.BlockSpec((tm, c), lambda i: (i, 0))
    vec = pl.BlockSpec((1, d), lambda i: (0, 0))
    if split:
        na = split_rows // tm
        out_specs = [pl.BlockSpec((tm, d), lambda i: (jnp.minimum(i, na - 1), 0)),
                     pl.BlockSpec((tm, d), lambda i: (jnp.maximum(i - na, 0), 0))]
        out_shape = [jax.ShapeDtypeStruct((split_rows, d), F32),
                     jax.ShapeDtypeStruct((t - split_rows, d), F32)]
    else:
        na = 0
        out_specs = [row(d), row(d)]
        out_shape = [jax.ShapeDtypeStruct((t, d), F32), jax.ShapeDtypeStruct((t, d), BF16)]
    return pl.pallas_call(
        functools.partial(_combine_kernel, tm, nch, alpha, na, split),
        grid=(nsteps,),
        in_specs=[pl.BlockSpec(memory_space=pl.ANY),
                  pl.BlockSpec((tm * nch, LANES), lambda i: (i, 0)), row(LANES), vec, vec,
                  pl.BlockSpec(memory_space=pl.ANY)],
        out_specs=out_specs,
        out_shape=out_shape,
        scratch_shapes=[pltpu.SMEM((2, tm * TOP_K), I32),
                        pltpu.VMEM((2, TOP_K, tm * nch, LANES), F32),
                        pltpu.SemaphoreType.DMA((2,)), pltpu.SemaphoreType.DMA((2,))],
        compiler_params=_cparams(1),
        name="moe_combine",
    )(dest2, h1_tt, wt128, g.reshape(1, d), b.reshape(1, d), y_tt)


def _route_plan(idx, rank, counts, n_blocks):
    padded = (counts + MOE_BLOCK - 1) // MOE_BLOCK * MOE_BLOCK
    pad_ends = jnp.cumsum(padded)
    pad_starts = pad_ends - padded
    dest = pad_starts[idx] + rank
    block_start = jnp.arange(n_blocks, dtype=I32) * MOE_BLOCK
    n_exp = counts.shape[0]
    block_e = jnp.minimum(jnp.sum(block_start[:, None] >= pad_ends[None, :], axis=1), n_exp - 1).astype(I32)
    n_used = (pad_ends[-1] // MOE_BLOCK).astype(I32).reshape(1)
    return dest.astype(I32), block_e, n_used, (pad_starts + counts).astype(I32), (padded - counts).astype(I32)


def _tile(n, pref):
    t = min(pref, n)
    while n % t:
        t //= 2
    return t


def kernel(x_prompt, x_sample, ln_in_g, ln_in_b, w_in, conv_w, conv_b, filt_w1, filt_b1, filt_freq1, filt_w2, filt_b2, filt_freq2, filt_w3, filt_decay, hy_bias, w_hy_out, w_fn_out, w_o, ln1_g, ln1_b, w_router, b_router, w_gu, b_gu, w_down, b_down, ln2_g, ln2_b):
    bp, lp, d = x_prompt.shape
    bs, ls, _ = x_sample.shape
    depth = w_in.shape[0]
    hy = hy_bias.shape[-1]
    fn = w_fn_out.shape[1]
    n_exp = w_router.shape[-1]
    tp, ts = bp * lp, bs * ls
    t = tp + ts
    alpha = (2 * depth) ** 0.25
    trunks = ((bp, lp), (bs, ls))
    n_blocks = -(-(t * TOP_K) // MOE_BLOCK) + n_exp
    n_slots = n_blocks * MOE_BLOCK

    tm_ln = _tile(math.gcd(tp, ts), 1024)
    tm_tok = _tile(math.gcd(lp, ls), 512)
    tm_row = _tile(math.gcd(tp, ts), 256)

    h, hb = _ln_in(x_prompt.reshape(tp, d), x_sample.reshape(ts, d), ln_in_g, ln_in_b, tm_ln)
    out = None
    for l in range(depth):
        x0, vv_a, vv_b, zf_a, zf_b, gates = _proj(hb, w_in[l].astype(BF16), conv_w[l], conv_b[l],
                                                  hy, fn, tp, lp, ls, tm_tok)
        filt_args = (filt_w1[l], filt_b1[l], filt_freq1[l], filt_w2[l], filt_b2[l], filt_freq2[l],
                     filt_w3[l], filt_decay[l])
        vvs, zfs = (vv_a, vv_b), (zf_a, zf_b)
        convs = [_hyena_conv(v, b, s, filt_args, 8) for v, (b, s) in zip(vvs, trunks)]
        yfns = [_fnet_mix(z, b, s, 8) for z, (b, s) in zip(zfs, trunks)]
        h1, idx128, wt128 = _merge(alpha, convs, vvs, yfns, x0, gates, h, hy_bias[l],
                                   w_hy_out[l].astype(BF16), w_fn_out[l].astype(BF16),
                                   w_o[l].astype(BF16), ln1_g[l], ln1_b[l], w_router[l], b_router[l],
                                   tm_tok)
        rank128, cnt128 = _ranks(idx128, tm_tok)
        dest, block_e, n_used, zstart, znum = _route_plan(
            idx128[:, :TOP_K], rank128[:, :TOP_K], cnt128[0, :n_exp], n_blocks)
        xs = _dispatch(zstart, znum, n_used, dest, h1, d // LANES, n_slots, tm_row)
        y = _experts(block_e, n_used, xs, l, w_gu, b_gu[l], w_down, b_down[l])
        if l + 1 < depth:
            h, hb = _combine(alpha, dest, h1, wt128, ln2_g[l], ln2_b[l], y, tm_row)
        else:
            out = _combine(alpha, dest, h1, wt128, ln2_g[l], ln2_b[l], y, tm_row, split_rows=tp)
    return (out[0].reshape(bp, lp, d), out[1].reshape(bs, ls, d))
```

```python
import functools
import math

import jax
import jax.numpy as jnp
from jax import lax
from jax.experimental import pallas as pl
from jax.experimental.pallas import tpu as pltpu

F32 = jnp.float32
BF16 = jnp.bfloat16
I32 = jnp.int32
FP8 = jnp.float8_e4m3fn
FP8_PEAK = 256.0

TOP_K = 4
FN_GROUPS = 4
MOE_BLOCK = 512
FILTER_BANDS = 16
MOD_SHIFT = 0.05
SWIGLU_LIMIT = 7.0
SWIGLU_ALPHA = 1.702
ACT_SCALE = 4.0
LN_EPS = 1e-5
LANES = 128
DFT_RADIX = 128
FNET_MINOR = 64
VMEM_LIMIT = 48 * 1024 * 1024
VMEM_LIMIT_EXPERTS = 56 * 1024 * 1024


def _cparams(ndim, vmem=VMEM_LIMIT):
    return pltpu.CompilerParams(dimension_semantics=("arbitrary",) * ndim,
                                vmem_limit_bytes=vmem)


def _store_token_tiles(ref, x):
    rows, d = x.shape
    nch = d // LANES
    for c in range(nch):
        ref[pl.ds(c, rows, stride=nch), :] = x[:, c * LANES:(c + 1) * LANES]


def _load_token_tiles(ref, rows, nch):
    return jnp.concatenate([ref[pl.ds(c, rows, stride=nch), :] for c in range(nch)], axis=1)


def _layer_norm(x, g, b):
    mu = jnp.mean(x, axis=-1, keepdims=True)
    xc = x - mu
    var = jnp.mean(xc * xc, axis=-1, keepdims=True)
    return xc * lax.rsqrt(var + LN_EPS) * g + b


def _ln_in_kernel(n_first, xa_ref, xb_ref, g_ref, b_ref, h_ref, hb_ref):
    i = pl.program_id(0)

    def emit(x_ref):
        y = _layer_norm(x_ref[...], g_ref[...], b_ref[...])
        h_ref[...] = y
        hb_ref[...] = y.astype(BF16)

    @pl.when(i < n_first)
    def _():
        emit(xa_ref)

    @pl.when(i >= n_first)
    def _():
        emit(xb_ref)


def _ln_in(xa, xb, g, b, tm):
    ta, d = xa.shape
    tb = xb.shape[0]
    na, nb = ta // tm, tb // tm
    t = ta + tb
    return pl.pallas_call(
        functools.partial(_ln_in_kernel, na),
        grid=(na + nb,),
        in_specs=[
            pl.BlockSpec((tm, d), lambda i: (jnp.minimum(i, na - 1), 0)),
            pl.BlockSpec((tm, d), lambda i: (jnp.maximum(i - na, 0), 0)),
            pl.BlockSpec((1, d), lambda i: (0, 0)),
            pl.BlockSpec((1, d), lambda i: (0, 0)),
        ],
        out_specs=[pl.BlockSpec((tm, d), lambda i: (i, 0)),
                   pl.BlockSpec((tm, d), lambda i: (i, 0))],
        out_shape=[jax.ShapeDtypeStruct((t, d), F32), jax.ShapeDtypeStruct((t, d), BF16)],
        compiler_params=_cparams(1),
        name="ln_in",
    )(xa, xb, g.reshape(1, d), b.reshape(1, d))


HALO = 16


def _proj_kernel(tm, t_first, l_first, l_second, hy, fn, x_ref, xp_ref, xn_ref, w_ref, cw_ref, cb_ref,
                 x0_ref, vva_ref, vvb_ref, zfa_ref, zfb_ref, gate_ref):
    i = pl.program_id(0)
    r0 = i * tm
    seq = jnp.where(r0 < t_first, l_first, l_second)
    is_first = lax.rem(r0, seq) == 0
    is_last = lax.rem(r0 + tm, seq) == 0
    x = x_ref[...]
    x_ext = jnp.concatenate([xp_ref[...], x, xn_ref[...]], axis=0)
    row = lax.broadcasted_iota(I32, (tm, 1), 0)
    kill_prev = jnp.logical_and(row == 0, is_first)
    kill_next = jnp.logical_and(row == tm - 1, is_last)

    def conv_chunk(c):
        lo, hi = c * hy, (c + 1) * hy
        z = jnp.dot(x_ext, w_ref[:, lo:hi], preferred_element_type=F32)
        um1 = jnp.where(kill_prev, 0.0, z[HALO - 1:HALO - 1 + tm])
        up1 = jnp.where(kill_next, 0.0, z[HALO + 1:HALO + 1 + tm])
        return (um1 * cw_ref[0:1, lo:hi] + z[HALO:HALO + tm] * cw_ref[1:2, lo:hi]
                + up1 * cw_ref[2:3, lo:hi] + cb_ref[:, lo:hi])

    x0_ref[...] = conv_chunk(0).astype(BF16)
    vv = (conv_chunk(2) * conv_chunk(1)).astype(BF16)
    zfn = jnp.dot(x, w_ref[:, 3 * hy:3 * hy + fn], preferred_element_type=F32).astype(BF16)

    @pl.when(r0 < t_first)
    def _():
        vva_ref[...] = vv
        zfa_ref[...] = zfn

    @pl.when(r0 >= t_first)
    def _():
        vvb_ref[...] = vv
        zfb_ref[...] = zfn

    g = jnp.dot(x, w_ref[:, 3 * hy + fn:], preferred_element_type=F32)
    gate_ref[...] = jax.nn.sigmoid(g).astype(BF16)


def _proj(hb, w_in, conv_w, conv_b, hy, fn, t_first, l_first, l_second, tm):
    t, d = hb.shape
    n = w_in.shape[1]
    ng = n - 3 * hy - fn
    nh = tm // HALO
    last = t // HALO - 1
    row = lambda c: pl.BlockSpec((tm, c), lambda i: (i, 0))
    na = t_first // tm
    row_a = lambda c: pl.BlockSpec((tm, c), lambda i: (jnp.minimum(i, na - 1), 0))
    row_b = lambda c: pl.BlockSpec((tm, c), lambda i: (jnp.maximum(i - na, 0), 0))
    bf = lambda r, c: jax.ShapeDtypeStruct((r, c), BF16)
    return pl.pallas_call(
        functools.partial(_proj_kernel, tm, t_first, l_first, l_second, hy, fn),
        grid=(t // tm,),
        in_specs=[
            row(d),
            pl.BlockSpec((HALO, d), lambda i: (jnp.maximum(i * nh - 1, 0), 0)),
            pl.BlockSpec((HALO, d), lambda i: (jnp.minimum((i + 1) * nh, last), 0)),
            pl.BlockSpec((d, n), lambda i: (0, 0)),
            pl.BlockSpec((3, 3 * hy), lambda i: (0, 0)),
            pl.BlockSpec((1, 3 * hy), lambda i: (0, 0)),
        ],
        out_specs=[row(hy), row_a(hy), row_b(hy), row_a(fn), row_b(fn), row(ng)],
        out_shape=[bf(t, hy), bf(t_first, hy), bf(t - t_first, hy),
                   bf(t_first, fn), bf(t - t_first, fn), bf(t, ng)],
        compiler_params=_cparams(1),
        name="proj_hyena_pre",
    )(hb, hb, hb, w_in, conv_w, conv_b.reshape(1, 3 * hy))


def _filter_kernel(tr, seq, w1_ref, b1_ref, f1_ref, w2_ref, b2_ref, f2_ref, w3_ref, dec_ref, o_ref):
    i = pl.program_id(0)
    n = i * tr + lax.broadcasted_iota(I32, (tr, 1), 0)
    m = jnp.where(n < seq, n, 2 * seq - n).astype(F32)
    t = m * (1.0 / (seq - 1))
    ang = (2.0 * math.pi / seq) * m
    band = lax.broadcasted_iota(I32, (1, FILTER_BANDS), 1).astype(F32)
    freqs = 1e-4 + band * ((FILTER_BANDS - 1 - 1e-4) / (FILTER_BANDS - 1))
    fa = ang * freqs
    hi = lax.Precision.HIGHEST
    pre = (t * w1_ref[0:1, :]
           + jnp.dot(jnp.cos(fa), w1_ref[1:1 + FILTER_BANDS, :], precision=hi, preferred_element_type=F32)
           - jnp.dot(jnp.sin(fa), w1_ref[1 + FILTER_BANDS:, :], precision=hi, preferred_element_type=F32))
    h = jnp.sin(f1_ref[...] * (pre + b1_ref[...]))
    h = jnp.sin(f2_ref[...] * (jnp.dot(h, w2_ref[...], precision=hi, preferred_element_type=F32)
                               + b2_ref[...]))
    h = jnp.dot(h, w3_ref[...], precision=hi, preferred_element_type=F32)
    h = h * (jnp.exp(-t * jnp.abs(dec_ref[...])) + MOD_SHIFT)
    o_ref[...] = jnp.where(n == seq, 0.0, h)


def _filter(seq, w1, b1, f1, w2, b2, f2, w3, dec, tr):
    hid = w1.shape[1]
    hy = w3.shape[1] // 2
    nfwd = seq // tr
    side = lambda i: (0, jnp.where(i >= nfwd, 1, 0))
    full = lambda i: (0, 0)
    return pl.pallas_call(
        functools.partial(_filter_kernel, tr, seq),
        grid=(2 * seq // tr,),
        in_specs=[
            pl.BlockSpec(w1.shape, full), pl.BlockSpec((1, hid), full), pl.BlockSpec((1, hid), full),
            pl.BlockSpec(w2.shape, full), pl.BlockSpec((1, hid), full), pl.BlockSpec((1, hid), full),
            pl.BlockSpec((hid, hy), side), pl.BlockSpec((1, hy), side),
        ],
        out_specs=pl.BlockSpec((tr, hy), lambda i: (i, 0)),
        out_shape=jax.ShapeDtypeStruct((2 * seq, hy), F32),
        compiler_params=_cparams(1),
        name="hyena_filter",
    )(w1, b1.reshape(1, hid), f1.reshape(1, hid), w2, b2.reshape(1, hid), f2.reshape(1, hid),
      w3, dec.reshape(1, 2 * hy))


def _blm_kernel(tj, shared, m_ref, x_ref, o_ref):
    for jj in range(tj):
        m = m_ref[0 if shared else jj]
        o_ref[0, jj] = jnp.dot(m, x_ref[0, jj], preferred_element_type=F32).astype(o_ref.dtype)


def _blm(m, x, tj, out_dtype=BF16):
    g, j, k, c = x.shape
    jm, mr, _ = m.shape
    shared = jm == 1
    tj = min(tj, j)
    m_spec = (pl.BlockSpec((1, mr, k), lambda jb, gb: (0, 0, 0)) if shared
              else pl.BlockSpec((tj, mr, k), lambda jb, gb: (jb, 0, 0)))
    return pl.pallas_call(
        functools.partial(_blm_kernel, tj, shared),
        grid=(j // tj, g),
        in_specs=[m_spec, pl.BlockSpec((1, tj, k, c), lambda jb, gb: (gb, jb, 0, 0))],
        out_specs=pl.BlockSpec((1, tj, mr, c), lambda jb, gb: (gb, jb, 0, 0)),
        out_shape=jax.ShapeDtypeStruct((g, j, mr, c), out_dtype),
        compiler_params=_cparams(2),
        name="dft_stage",
    )(m, x)


def _fnet_s1_kernel(tj, fn, m_ref, r_ref, x_ref, o_ref):
    for jj in range(tj):
        z = jnp.dot(x_ref[0, jj], r_ref[...], preferred_element_type=F32)
        zs = jnp.concatenate([z[:, :fn], z[:, fn:]], axis=0).astype(BF16)
        o_ref[0, jj] = jnp.dot(m_ref[jj], zs, preferred_element_type=F32).astype(o_ref.dtype)


def _fnet_s1(m, r, x, tj):
    g, j, k, fn = x.shape
    _, mr, k2 = m.shape
    tj = min(tj, j)
    return pl.pallas_call(
        functools.partial(_fnet_s1_kernel, tj, fn),
        grid=(j // tj, g),
        in_specs=[pl.BlockSpec((tj, mr, k2), lambda jb, gb: (jb, 0, 0)),
                  pl.BlockSpec(r.shape, lambda jb, gb: (0, 0)),
                  pl.BlockSpec((1, tj, k, fn), lambda jb, gb: (gb, jb, 0, 0))],
        out_specs=pl.BlockSpec((1, tj, mr, fn), lambda jb, gb: (gb, jb, 0, 0)),
        out_shape=jax.ShapeDtypeStruct((g, j, mr, fn), BF16),
        compiler_params=_cparams(2),
        name="fnet_stage1",
    )(m, r, x)


def _hyena_mid_kernel(tk, n2, f_ref, kf_ref, gi_ref, x_ref, o_ref):
    for kk in range(tk):
        a = jnp.dot(f_ref[...], x_ref[0, kk], preferred_element_type=F32)
        ar, ai = a[:n2], a[n2:]
        kr = kf_ref[kk, :n2].astype(F32)
        ki = kf_ref[kk, n2:].astype(F32)
        prod = jnp.concatenate([ar * kr - ai * ki, ar * ki + ai * kr], axis=0).astype(BF16)
        o_ref[0, kk] = jnp.dot(gi_ref[kk], prod, preferred_element_type=F32).astype(o_ref.dtype)


def _hyena_mid(f2s, kf, ginv, x, tk):
    g, n1, r, c = x.shape
    n2 = r // 2
    tk = min(tk, n1)
    return pl.pallas_call(
        functools.partial(_hyena_mid_kernel, tk, n2),
        grid=(n1 // tk, g),
        in_specs=[pl.BlockSpec((r, r), lambda kb, gb: (0, 0)),
                  pl.BlockSpec((tk, r, c), lambda kb, gb: (kb, 0, 0)),
                  pl.BlockSpec((tk, r, r), lambda kb, gb: (kb, 0, 0)),
                  pl.BlockSpec((1, tk, r, c), lambda kb, gb: (gb, kb, 0, 0))],
        out_specs=pl.BlockSpec((1, tk, r, c), lambda kb, gb: (gb, kb, 0, 0)),
        out_shape=jax.ShapeDtypeStruct(x.shape, BF16),
        compiler_params=_cparams(2),
        name="hyena_mid",
    )(f2s, kf, ginv, x)


def _cis(num, den, sign):
    ang = (2.0 * math.pi / den) * lax.rem(num, den).astype(F32)
    return jnp.cos(ang), sign * jnp.sin(ang)


def _stack_complex(mr, mi):
    top = jnp.concatenate([mr, -mi], axis=-1)
    bot = jnp.concatenate([mi, mr], axis=-1)
    return jnp.concatenate([top, bot], axis=-2)


def _iota(shape, axis):
    return lax.broadcasted_iota(I32, shape, axis)


def _hyena_mats(n, n1, n2):
    n1h = n1 // 2
    sh = (n2, n1, n1)
    gr, gi = _cis(_iota(sh, 1) * (n2 * _iota(sh, 2) + _iota(sh, 0)), n, -1.0)
    g_data = _stack_complex(gr[:, :, :n1h], gi[:, :, :n1h]).astype(BF16)
    g_filt = jnp.concatenate([gr, gi], axis=1).astype(BF16)
    sh = (n2, n2)
    fr, fi = _cis(_iota(sh, 0) * _iota(sh, 1), n2, -1.0)
    f2s = _stack_complex(fr, fi).astype(BF16)
    sh = (n1, n2, n2)
    ir, ii = _cis(_iota(sh, 1) * (_iota(sh, 0) + n1 * _iota(sh, 2)), n, 1.0)
    ginv = _stack_complex(ir, ii).astype(BF16)
    sh = (n1h, n1)
    br, bi = _cis(_iota(sh, 0) * _iota(sh, 1), n1, 1.0)
    f1inv = (_stack_complex(br, bi) * (1.0 / n)).astype(BF16)[None]
    return g_data, g_filt, f2s, ginv, f1inv


def _fnet_mats(seq, na, nb, fn):
    gd = fn // FN_GROUPS
    sh = (nb, na, na)
    gr, gi = _cis(_iota(sh, 1) * (nb * _iota(sh, 2) + _iota(sh, 0)), seq, -1.0)
    g1 = _stack_complex(gr, gi).astype(BF16)
    sh = (nb, nb)
    fr, fi = _cis(_iota(sh, 0) * _iota(sh, 1), nb, -1.0)
    scale = 1.0 / math.sqrt(seq * gd)
    f2re = (jnp.concatenate([fr, -fi], axis=-1) * scale).astype(BF16)[None]
    sh = (fn, fn)
    same = (_iota(sh, 0) // gd) == (_iota(sh, 1) // gd)
    cr, ci = _cis(_iota(sh, 0) * _iota(sh, 1), gd, -1.0)
    rmat = jnp.concatenate([jnp.where(same, cr, 0.0), jnp.where(same, ci, 0.0)], axis=1).astype(BF16)
    return g1, f2re, rmat


def _hyena_conv(vv, batch, seq, filt_args, tj):
    c = vv.shape[1]
    n = 2 * seq
    n1 = DFT_RADIX
    n2 = n // n1
    n1h = n1 // 2
    bp = batch // 2
    g_data, g_filt, f2s, ginv, f1inv = _hyena_mats(n, n1, n2)
    kt = _filter(seq, *filt_args, tr=min(512, seq))
    kt = kt.reshape(1, n1, n2, c).transpose(0, 2, 1, 3).astype(BF16)
    ks = _blm(g_filt, kt, tj)
    ks = ks.reshape(1, n2, 2, n1, c).transpose(0, 3, 2, 1, 4).reshape(1, n1, 2 * n2, c)
    kf = _blm(f2s[None], ks, tj, out_dtype=F32)[0]
    x = vv.reshape(bp, 2, n1h, n2, c).transpose(0, 3, 1, 2, 4).reshape(bp, n2, n1, c)
    a = _blm(g_data, x, tj)
    a = a.reshape(bp, n2, 2, n1, c).transpose(0, 3, 2, 1, 4).reshape(bp, n1, 2 * n2, c)
    z = _hyena_mid(f2s, kf, ginv, a, 4)
    z = z.reshape(bp, n1, 2, n2, c).transpose(0, 3, 2, 1, 4).reshape(bp, n2, 2 * n1, c)
    y = _blm(f1inv, z, tj)
    y = y.reshape(bp, n2, 2, n1h, c).transpose(0, 2, 3, 1, 4)
    return y.reshape(batch * seq, c)


def _fnet_mix(u, batch, seq, tj):
    fn = u.shape[1]
    nb = FNET_MINOR
    na = seq // nb
    g1, f2re, rmat = _fnet_mats(seq, na, nb, fn)
    x = u.reshape(batch, na, nb, fn).transpose(0, 2, 1, 3)
    a = _fnet_s1(g1, rmat, x, tj)
    a = a.reshape(batch, nb, 2, na, fn).transpose(0, 3, 2, 1, 4).reshape(batch, na, 2 * nb, fn)
    y = _blm(f2re, a, tj)
    return y.transpose(0, 2, 1, 3).reshape(batch * seq, fn)


def _merge_kernel(alpha, d, n_exp, n_first, conva_ref, convb_ref, vva_ref, vvb_ref, yfna_ref, yfnb_ref,
                  x0_ref, gate_ref, h_ref, bias_ref,
                  why_ref, wfn_ref, wo_ref, g_ref, b_ref, wr_ref, br_ref,
                  h1_ref, idx_ref, wt_ref):
    in_first = pl.program_id(0) < n_first
    pick = lambda a_ref, b_ref: jnp.where(in_first, a_ref[...], b_ref[...])
    vv = pick(vva_ref, vvb_ref).astype(F32)
    y_hy = x0_ref[...].astype(F32) * (pick(conva_ref, convb_ref).astype(F32) + vv * bias_ref[...])
    a = jnp.dot(y_hy.astype(BF16), why_ref[...], preferred_element_type=F32)
    bfn = jnp.dot(pick(yfna_ref, yfnb_ref), wfn_ref[...], preferred_element_type=F32)
    gates = gate_ref[...].astype(F32)
    merged = gates[:, :d] * a + gates[:, d:] * bfn
    m = jnp.dot(merged.astype(BF16), wo_ref[...], preferred_element_type=F32)
    h1 = _layer_norm(alpha * h_ref[...] + m, g_ref[...], b_ref[...])
    _store_token_tiles(h1_ref, h1)
    logits = jnp.dot(h1.astype(BF16), wr_ref[...], preferred_element_type=F32) + br_ref[...]
    lane = lax.broadcasted_iota(I32, logits.shape, 1)
    lane_f = lane.astype(F32)
    neg = jnp.float32(-jnp.inf)
    logits = jnp.where(lane < n_exp, logits, neg)
    vals, idxs = [], []
    for _ in range(TOP_K):
        mx = jnp.max(logits, axis=-1, keepdims=True)
        ix = jnp.min(jnp.where(logits == mx, lane_f, float(LANES)), axis=-1, keepdims=True).astype(I32)
        vals.append(mx)
        idxs.append(ix)
        logits = jnp.where(lane == ix, neg, logits)
    exps = [jnp.exp(v - vals[0]) for v in vals]
    den = exps[0]
    for e in exps[1:]:
        den = den + e
    idx_out = jnp.zeros(lane.shape, I32)
    wt_out = jnp.zeros(lane.shape, F32)
    for k in range(TOP_K):
        idx_out = jnp.where(lane == k, idxs[k], idx_out)
        wt_out = jnp.where(lane == k, exps[k] / den, wt_out)
    idx_ref[...] = idx_out
    wt_ref[...] = wt_out


def _merge(alpha, convs, vvs, yfns, x0, gates, h, bias, why, wfn, wo, g, b, wr, br, tm):
    t, d = h.shape
    hy = x0.shape[1]
    fn = yfns[0].shape[1]
    n_exp = wr.shape[1]
    na = convs[0].shape[0] // tm
    wr_p = jnp.zeros((d, LANES), BF16).at[:, :n_exp].set(wr.astype(BF16))
    br_p = jnp.zeros((1, LANES), F32).at[0, :n_exp].set(br)
    row = lambda c: pl.BlockSpec((tm, c), lambda i: (i, 0))
    row_a = lambda c: pl.BlockSpec((tm, c), lambda i: (jnp.minimum(i, na - 1), 0))
    row_b = lambda c: pl.BlockSpec((tm, c), lambda i: (jnp.maximum(i - na, 0), 0))
    full = lambda a: pl.BlockSpec(a.shape, lambda i: (0,) * a.ndim)
    args = (convs[0], convs[1], vvs[0], vvs[1], yfns[0], yfns[1], x0, gates, h,
            bias.reshape(1, hy), why, wfn, wo, g.reshape(1, d), b.reshape(1, d), wr_p, br_p)
    in_specs = ([row_a(hy), row_b(hy), row_a(hy), row_b(hy), row_a(fn), row_b(fn),
                 row(hy), row(2 * d), row(d)] + [full(a) for a in args[9:]])
    return pl.pallas_call(
        functools.partial(_merge_kernel, alpha, d, n_exp, na),
        grid=(t // tm,),
        in_specs=in_specs,
        out_specs=[pl.BlockSpec((tm * (d // LANES), LANES), lambda i: (i, 0)), row(LANES), row(LANES)],
        out_shape=[jax.ShapeDtypeStruct((t * (d // LANES), LANES), F32),
                   jax.ShapeDtypeStruct((t, LANES), I32), jax.ShapeDtypeStruct((t, LANES), F32)],
        compiler_params=_cparams(1),
        name="merge_route",
    )(*args)


def _rank_kernel(tm, idx_ref, rank_ref, cnt_ref, carry_ref):
    i = pl.program_id(0)

    @pl.when(i == 0)
    def _():
        carry_ref[...] = jnp.zeros_like(carry_ref)

    idx = idx_ref[...]
    lane = lax.broadcasted_iota(I32, idx.shape, 1)
    cols = [idx[:, k:k + 1] for k in range(TOP_K)]
    onehot = jnp.zeros(idx.shape, F32)
    for ck in cols:
        onehot = onehot + (lane == ck).astype(F32)
    r = lax.broadcasted_iota(I32, (tm, tm), 0)
    c = lax.broadcasted_iota(I32, (tm, tm), 1)
    tri = (c < r).astype(BF16)
    before = jnp.dot(tri, onehot.astype(BF16), preferred_element_type=F32) + carry_ref[...]
    out = jnp.zeros(idx.shape, F32)
    for k, ck in enumerate(cols):
        rk = jnp.sum(jnp.where(lane == ck, before, 0.0), axis=-1, keepdims=True)
        out = jnp.where(lane == k, rk, out)
    rank_ref[...] = out.astype(I32)
    total = carry_ref[...] + jnp.sum(onehot, axis=0, keepdims=True)
    carry_ref[...] = total
    cnt_ref[...] = total.astype(I32)


def _ranks(idx128, tm):
    t = idx128.shape[0]
    return pl.pallas_call(
        functools.partial(_rank_kernel, tm),
        grid=(t // tm,),
        in_specs=[pl.BlockSpec((tm, LANES), lambda i: (i, 0))],
        out_specs=[pl.BlockSpec((tm, LANES), lambda i: (i, 0)),
                   pl.BlockSpec((1, LANES), lambda i: (0, 0))],
        out_shape=[jax.ShapeDtypeStruct((t, LANES), I32), jax.ShapeDtypeStruct((1, LANES), I32)],
        scratch_shapes=[pltpu.VMEM((1, LANES), F32)],
        compiler_params=_cparams(1),
        name="expert_rank",
    )(idx128)


def _token_copy(nch, src_ref, src_tok, dst_ref, dst_tok, sem):
    rows = lambda tok: pl.ds(tok * nch if isinstance(tok, int) else pl.multiple_of(tok * nch, nch), nch)
    return pltpu.make_async_copy(src_ref.at[rows(src_tok)], dst_ref.at[rows(dst_tok)], sem)


N_TILE_BUF = 3


def _dispatch_kernel(tm, nch, n_exp, n_blocks, zstart_ref, znum_ref, nused_ref, dest_hbm, h_hbm, xs_hbm,
                     dsm, hbuf, zbuf, sem_idx, sem_tile, sem_row, sem_z):
    i = pl.program_id(0)
    n = pl.num_programs(0)
    slot = lax.rem(i, 2)
    tb = lax.rem(i, N_TILE_BUF)
    tile_rows = tm * nch
    blk_rows = MOE_BLOCK * nch

    def idx_fetch(step, s):
        return pltpu.make_async_copy(dest_hbm.at[step], dsm.at[s], sem_idx.at[s])

    def tile_load(step, b):
        row = pl.multiple_of(step * tile_rows, tile_rows)
        return pltpu.make_async_copy(h_hbm.at[pl.ds(row, tile_rows)], hbuf.at[b], sem_tile.at[b])

    def rows_done(s):
        return pltpu.make_async_copy(hbuf.at[0], xs_hbm.at[pl.ds(0, tile_rows)], sem_row.at[s])

    @pl.when(i == 0)
    def _():
        idx_fetch(0, 0).start()
        tile_load(0, 0).start()

    idx_fetch(i, slot).wait()
    tile_load(i, tb).wait()

    @pl.when(i + 1 < n)
    def _():
        idx_fetch(i + 1, 1 - slot).start()
        tile_load(i + 1, lax.rem(i + 1, N_TILE_BUF)).start()

    for r in range(tm):
        for k in range(TOP_K):
            _token_copy(nch, hbuf.at[tb], r, xs_hbm, dsm[slot, r * TOP_K + k],
                        sem_row.at[slot]).start(priority=k % 2)

    @pl.when(i > 0)
    def _():
        for _ in range(TOP_K):
            rows_done(1 - slot).wait()

    @pl.when(i == n - 1)
    def _():
        for _ in range(TOP_K):
            rows_done(slot).wait()
        zbuf[...] = jnp.zeros_like(zbuf)

        def per_expert(e, carry):
            start = zstart_ref[e]
            num = znum_ref[e]

            def zissue(r, c2):
                _token_copy(nch, zbuf, 0, xs_hbm, start + r, sem_z).start()
                return c2

            def zwait(r, c2):
                _token_copy(nch, zbuf, 0, xs_hbm, 0, sem_z).wait()
                return c2

            lax.fori_loop(0, num, zissue, 0)
            lax.fori_loop(0, num, zwait, 0)
            return carry

        lax.fori_loop(0, n_exp, per_expert, 0)

        def blk_copy(bi):
            row = pl.multiple_of(bi * blk_rows, blk_rows)
            return pltpu.make_async_copy(zbuf, xs_hbm.at[pl.ds(row, blk_rows)], sem_z)

        def bissue(bi, carry):
            blk_copy(bi).start()
            return carry

        def bwait(bi, carry):
            blk_copy(bi).wait()
            return carry

        lax.fori_loop(nused_ref[0], n_blocks, bissue, 0)
        lax.fori_loop(nused_ref[0], n_blocks, bwait, 0)


def _dispatch(zstart, znum, n_used, dest, h1_tt, nch, n_slots, tm):
    t = h1_tt.shape[0] // nch
    n_exp = zstart.shape[0]
    dest2 = dest.reshape(t // tm, tm * TOP_K)
    return pl.pallas_call(
        functools.partial(_dispatch_kernel, tm, nch, n_exp, n_slots // MOE_BLOCK),
        grid_spec=pltpu.PrefetchScalarGridSpec(
            num_scalar_prefetch=3,
            grid=(t // tm,),
            in_specs=[pl.BlockSpec(memory_space=pl.ANY), pl.BlockSpec(memory_space=pl.ANY)],
            out_specs=pl.BlockSpec(memory_space=pl.ANY),
            scratch_shapes=[pltpu.SMEM((2, tm * TOP_K), I32),
                            pltpu.VMEM((N_TILE_BUF, tm * nch, LANES), F32),
                            pltpu.VMEM((MOE_BLOCK * nch, LANES), F32),
                            pltpu.SemaphoreType.DMA((2,)), pltpu.SemaphoreType.DMA((N_TILE_BUF,)),
                            pltpu.SemaphoreType.DMA((2,)), pltpu.SemaphoreType.DMA],
        ),
        out_shape=jax.ShapeDtypeStruct((n_slots * nch, LANES), F32),
        compiler_params=_cparams(1),
        name="moe_dispatch",
    )(zstart, znum, n_used, dest2, h1_tt)


def _expert_kernel(d_ff, nch, be_ref, nused_ref, xs_ref, x_ref, wgu_ref, bgu_ref, wd_ref, bd_ref, y_ref,
                   wgu_q, wd_q, sgu_ref, sd_ref):
    i = pl.program_id(0)
    live = i < nused_ref[0]
    prev = be_ref[jnp.maximum(i - 1, 0)]

    def amax(v):
        return jnp.max(jnp.max(jnp.abs(v), axis=0, keepdims=True), axis=1, keepdims=True)

    def quant(v):
        m = jnp.maximum(amax(v), 1e-30)
        return (v * (FP8_PEAK / m)).astype(FP8), m * (1.0 / FP8_PEAK)

    @pl.when(jnp.logical_and(live, jnp.logical_or(i == 0, be_ref[i] != prev)))
    def _():
        wgu_q[...], sgu_ref[...] = quant(wgu_ref[0])
        wd_q[...], sd_ref[...] = quant(wd_ref[0])

    @pl.when(live)
    def _():
        x = (_load_token_tiles(x_ref, MOE_BLOCK, nch) * xs_ref[0:1, 0:1]).astype(FP8)
        gu = (jnp.dot(x, wgu_q[...], preferred_element_type=F32) * (xs_ref[0:1, 1:2] * sgu_ref[...])
              + bgu_ref[0])
        gu = gu.astype(BF16)
        gate = jnp.minimum(gu[:, :d_ff], SWIGLU_LIMIT)
        up = jnp.clip(gu[:, d_ff:], -SWIGLU_LIMIT, SWIGLU_LIMIT)
        act = gate * jax.nn.sigmoid(SWIGLU_ALPHA * gate) * (up + 1.0)
        a = (act * ACT_SCALE).astype(FP8)
        y = jnp.dot(a, wd_q[...], preferred_element_type=F32) * (sd_ref[...] * (1.0 / ACT_SCALE)) + bd_ref[0]
        _store_token_tiles(y_ref, y)

    @pl.when(i >= nused_ref[0])
    def _():
        y_ref[...] = jnp.zeros_like(y_ref)


def _experts(block_e, n_used, xs_tt, x_bound, layer, wgu_all, bgu, wd_all, bd):
    depth, n_exp, d, two_ff = wgu_all.shape
    x_scale = FP8_PEAK / jnp.maximum(x_bound, 1e-30)
    x_scales = jnp.zeros((1, LANES), F32).at[0, 0].set(x_scale).at[0, 1].set(1.0 / x_scale)
    nch = d // LANES
    n_slots = xs_tt.shape[0] // nch
    d_ff = two_ff // 2
    nb = n_slots // MOE_BLOCK
    live = lambda i, be, nu: jnp.minimum(i, nu[0] - 1)
    wgu = wgu_all.reshape(depth * n_exp, d, two_ff)
    wd = wd_all.reshape(depth * n_exp, d_ff, d)
    w_row = lambda i, be, nu: (layer * n_exp + be[live(i, be, nu)], 0, 0)
    return pl.pallas_call(
        functools.partial(_expert_kernel, d_ff, nch),
        grid_spec=pltpu.PrefetchScalarGridSpec(
            num_scalar_prefetch=2,
            grid=(nb,),
            in_specs=[
                pl.BlockSpec((1, LANES), lambda i, be, nu: (0, 0)),
                pl.BlockSpec((MOE_BLOCK * nch, LANES), lambda i, be, nu: (live(i, be, nu), 0)),
                pl.BlockSpec((1, d, two_ff), w_row),
                pl.BlockSpec((1, 1, two_ff), lambda i, be, nu: (be[live(i, be, nu)], 0, 0)),
                pl.BlockSpec((1, d_ff, d), w_row),
                pl.BlockSpec((1, 1, d), lambda i, be, nu: (be[live(i, be, nu)], 0, 0)),
            ],
            out_specs=pl.BlockSpec((MOE_BLOCK * nch, LANES), lambda i, be, nu: (i, 0)),
            scratch_shapes=[pltpu.VMEM((d, two_ff), FP8), pltpu.VMEM((d_ff, d), FP8),
                            pltpu.VMEM((1, 1), F32), pltpu.VMEM((1, 1), F32)],
        ),
        out_shape=jax.ShapeDtypeStruct((n_slots * nch, LANES), F32),
        compiler_params=_cparams(1, VMEM_LIMIT_EXPERTS),
        name="moe_experts",
    )(block_e, n_used, x_scales, xs_tt, wgu, bgu.reshape(n_exp, 1, two_ff), wd,
      bd.reshape(n_exp, 1, d))


def _combine_kernel(tm, nch, alpha, n_first, split, dest_hbm, h1_ref, wt_ref, g_ref, b_ref, y_hbm,
                    *rest):
    if split:
        oa_ref, ob_ref, dsm, ybuf, sem_idx, sem = rest
    else:
        o_ref, ob16_ref, dsm, ybuf, sem_idx, sem = rest
    i = pl.program_id(0)
    n = pl.num_programs(0)
    slot = lax.rem(i, 2)

    def idx_fetch(step, s):
        return pltpu.make_async_copy(dest_hbm.at[step], dsm.at[s], sem_idx.at[s])

    def gather(s):
        for r in range(tm):
            for k in range(TOP_K):
                _token_copy(nch, y_hbm, dsm[s, r * TOP_K + k], ybuf.at[s, k], r,
                            sem.at[s]).start(priority=k % 2)

    @pl.when(i == 0)
    def _():
        first = idx_fetch(0, 0)
        first.start()
        first.wait()
        gather(0)

        @pl.when(n > 1)
        def _():
            idx_fetch(1, 1).start()

    @pl.when(i + 1 < n)
    def _():
        idx_fetch(i + 1, 1 - slot).wait()
        gather(1 - slot)

    @pl.when(i + 2 < n)
    def _():
        idx_fetch(i + 2, slot).start()

    for k in range(TOP_K):
        pltpu.make_async_copy(y_hbm.at[pl.ds(0, tm * nch)], ybuf.at[slot, k], sem.at[slot]).wait()
    wt = wt_ref[...]
    f = wt[:, 0:1] * _load_token_tiles(ybuf.at[slot, 0], tm, nch)
    for k in range(1, TOP_K):
        f = f + wt[:, k:k + 1] * _load_token_tiles(ybuf.at[slot, k], tm, nch)
    h1 = _load_token_tiles(h1_ref, tm, nch)
    out = _layer_norm(alpha * h1 + f, g_ref[...], b_ref[...])
    if split:
        @pl.when(i < n_first)
        def _():
            oa_ref[...] = out

        @pl.when(i >= n_first)
        def _():
            ob_ref[...] = out
    else:
        o_ref[...] = out
        ob16_ref[...] = out.astype(BF16)


def _combine(alpha, dest, h1_tt, wt128, g, b, y_tt, tm, split_rows=None):
    d = g.shape[0]
    nch = d // LANES
    t = h1_tt.shape[0] // nch
    nsteps = t // tm
    dest2 = dest.reshape(nsteps, tm * TOP_K)
    split = split_rows is not None
    row = lambda c: pl.BlockSpec((tm, c), lambda i: (i, 0))
    vec = pl.BlockSpec((1, d), lambda i: (0, 0))
    if split:
        na = split_rows // tm
        out_specs = [pl.BlockSpec((tm, d), lambda i: (jnp.minimum(i, na - 1), 0)),
                     pl.BlockSpec((tm, d), lambda i: (jnp.maximum(i - na, 0), 0))]
        out_shape = [jax.ShapeDtypeStruct((split_rows, d), F32),
                     jax.ShapeDtypeStruct((t - split_rows, d), F32)]
    else:
        na = 0
        out_specs = [row(d), row(d)]
        out_shape = [jax.ShapeDtypeStruct((t, d), F32), jax.ShapeDtypeStruct((t, d), BF16)]
    return pl.pallas_call(
        functools.partial(_combine_kernel, tm, nch, alpha, na, split),
        grid=(nsteps,),
        in_specs=[pl.BlockSpec(memory_space=pl.ANY),
                  pl.BlockSpec((tm * nch, LANES), lambda i: (i, 0)), row(LANES), vec, vec,
                  pl.BlockSpec(memory_space=pl.ANY)],
        out_specs=out_specs,
        out_shape=out_shape,
        scratch_shapes=[pltpu.SMEM((2, tm * TOP_K), I32),
                        pltpu.VMEM((2, TOP_K, tm * nch, LANES), F32),
                        pltpu.SemaphoreType.DMA((2,)), pltpu.SemaphoreType.DMA((2,))],
        compiler_params=_cparams(1),
        name="moe_combine",
    )(dest2, h1_tt, wt128, g.reshape(1, d), b.reshape(1, d), y_tt)


def _route_plan(idx, rank, counts, n_blocks):
    padded = (counts + MOE_BLOCK - 1) // MOE_BLOCK * MOE_BLOCK
    pad_ends = jnp.cumsum(padded)
    pad_starts = pad_ends - padded
    dest = pad_starts[idx] + rank
    block_start = jnp.arange(n_blocks, dtype=I32) * MOE_BLOCK
    n_exp = counts.shape[0]
    block_e = jnp.minimum(jnp.sum(block_start[:, None] >= pad_ends[None, :], axis=1), n_exp - 1).astype(I32)
    n_used = (pad_ends[-1] // MOE_BLOCK).astype(I32).reshape(1)
    return dest.astype(I32), block_e, n_used, (pad_starts + counts).astype(I32), (padded - counts).astype(I32)


def _tile(n, pref):
    t = min(pref, n)
    while n % t:
        t //= 2
    return t


def kernel(x_prompt, x_sample, ln_in_g, ln_in_b, w_in, conv_w, conv_b, filt_w1, filt_b1, filt_freq1, filt_w2, filt_b2, filt_freq2, filt_w3, filt_decay, hy_bias, w_hy_out, w_fn_out, w_o, ln1_g, ln1_b, w_router, b_router, w_gu, b_gu, w_down, b_down, ln2_g, ln2_b):
    bp, lp, d = x_prompt.shape
    bs, ls, _ = x_sample.shape
    depth = w_in.shape[0]
    hy = hy_bias.shape[-1]
    fn = w_fn_out.shape[1]
    n_exp = w_router.shape[-1]
    tp, ts = bp * lp, bs * ls
    t = tp + ts
    alpha = (2 * depth) ** 0.25
    trunks = ((bp, lp), (bs, ls))
    n_blocks = -(-(t * TOP_K) // MOE_BLOCK) + n_exp
    n_slots = n_blocks * MOE_BLOCK

    tm_ln = _tile(math.gcd(tp, ts), 1024)
    tm_tok = _tile(math.gcd(lp, ls), 512)
    tm_row = _tile(math.gcd(tp, ts), 256)

    h, hb = _ln_in(x_prompt.reshape(tp, d), x_sample.reshape(ts, d), ln_in_g, ln_in_b, tm_ln)
    out = None
    for l in range(depth):
        x0, vv_a, vv_b, zf_a, zf_b, gates = _proj(hb, w_in[l].astype(BF16), conv_w[l], conv_b[l],
                                                  hy, fn, tp, lp, ls, tm_tok)
        filt_args = (filt_w1[l], filt_b1[l], filt_freq1[l], filt_w2[l], filt_b2[l], filt_freq2[l],
                     filt_w3[l], filt_decay[l])
        vvs, zfs = (vv_a, vv_b), (zf_a, zf_b)
        convs = [_hyena_conv(v, b, s, filt_args, 8) for v, (b, s) in zip(vvs, trunks)]
        yfns = [_fnet_mix(z, b, s, 8) for z, (b, s) in zip(zfs, trunks)]
        h1, idx128, wt128 = _merge(alpha, convs, vvs, yfns, x0, gates, h, hy_bias[l],
                                   w_hy_out[l].astype(BF16), w_fn_out[l].astype(BF16),
                                   w_o[l].astype(BF16), ln1_g[l], ln1_b[l], w_router[l], b_router[l],
                                   tm_tok)
        rank128, cnt128 = _ranks(idx128, tm_tok)
        dest, block_e, n_used, zstart, znum = _route_plan(
            idx128[:, :TOP_K], rank128[:, :TOP_K], cnt128[0, :n_exp], n_blocks)
        xs = _dispatch(zstart, znum, n_used, dest, h1, d // LANES, n_slots, tm_row)
        x_bound = math.sqrt(d) * jnp.max(jnp.abs(ln1_g[l])) + jnp.max(jnp.abs(ln1_b[l]))
        y = _experts(block_e, n_used, xs, x_bound, l, w_gu, b_gu[l], w_down, b_down[l])
        if l + 1 < depth:
            h, hb = _combine(alpha, dest, h1, wt128, ln2_g[l], ln2_b[l], y, tm_row)
        else:
            out = _combine(alpha, dest, h1, wt128, ln2_g[l], ln2_b[l], y, tm_row, split_rows=tp)
    return (out[0].reshape(bp, lp, d), out[1].reshape(bs, ls, d))
```

```python
import functools
import math

import jax
import jax.numpy as jnp
from jax import lax
from jax.experimental import pallas as pl
from jax.experimental.pallas import tpu as pltpu

F32 = jnp.float32
BF16 = jnp.bfloat16
I32 = jnp.int32
FP8 = jnp.float8_e4m3fn
FP8_PEAK = 256.0

TOP_K = 4
FN_GROUPS = 4
MOE_BLOCK = 512
FILTER_BANDS = 16
MOD_SHIFT = 0.05
SWIGLU_LIMIT = 7.0
SWIGLU_ALPHA = 1.702
ACT_SCALE = 4.0
LN_EPS = 1e-5
LANES = 128
SUB = 8
DFT_RADIX = 128
FNET_MINOR = 64
VMEM_LIMIT = 48 * 1024 * 1024
VMEM_LIMIT_EXPERTS = 56 * 1024 * 1024


def _cparams(ndim, vmem=VMEM_LIMIT):
    return pltpu.CompilerParams(dimension_semantics=("arbitrary",) * ndim,
                                vmem_limit_bytes=vmem)


def _store_token_tiles(ref, x):
    rows, d = x.shape
    nch = d // LANES
    for c in range(nch):
        ref[pl.ds(c, rows, stride=nch), :] = x[:, c * LANES:(c + 1) * LANES]


def _load_token_tiles(ref, rows, nch):
    return jnp.concatenate([ref[pl.ds(c, rows, stride=nch), :] for c in range(nch)], axis=1)


def _layer_norm(x, g, b):
    mu = jnp.mean(x, axis=-1, keepdims=True)
    xc = x - mu
    var = jnp.mean(xc * xc, axis=-1, keepdims=True)
    return xc * lax.rsqrt(var + LN_EPS) * g + b


def _ln_in_kernel(n_first, xa_ref, xb_ref, g_ref, b_ref, h_ref, hb_ref):
    i = pl.program_id(0)

    def emit(x_ref):
        y = _layer_norm(x_ref[...], g_ref[...], b_ref[...])
        h_ref[...] = y
        hb_ref[...] = y.astype(BF16)

    @pl.when(i < n_first)
    def _():
        emit(xa_ref)

    @pl.when(i >= n_first)
    def _():
        emit(xb_ref)


def _ln_in(xa, xb, g, b, tm):
    ta, d = xa.shape
    tb = xb.shape[0]
    na, nb = ta // tm, tb // tm
    t = ta + tb
    return pl.pallas_call(
        functools.partial(_ln_in_kernel, na),
        grid=(na + nb,),
        in_specs=[
            pl.BlockSpec((tm, d), lambda i: (jnp.minimum(i, na - 1), 0)),
            pl.BlockSpec((tm, d), lambda i: (jnp.maximum(i - na, 0), 0)),
            pl.BlockSpec((1, d), lambda i: (0, 0)),
            pl.BlockSpec((1, d), lambda i: (0, 0)),
        ],
        out_specs=[pl.BlockSpec((tm, d), lambda i: (i, 0)),
                   pl.BlockSpec((tm, d), lambda i: (i, 0))],
        out_shape=[jax.ShapeDtypeStruct((t, d), F32), jax.ShapeDtypeStruct((t, d), BF16)],
        compiler_params=_cparams(1),
        name="ln_in",
    )(xa, xb, g.reshape(1, d), b.reshape(1, d))


HALO = 16


def _proj_kernel(tm, t_first, l_first, l_second, hy, fn, x_ref, xp_ref, xn_ref, w_ref, cw_ref, cb_ref,
                 x0_ref, vva_ref, vvb_ref, zfa_ref, zfb_ref, gate_ref):
    i = pl.program_id(0)
    r0 = i * tm
    seq = jnp.where(r0 < t_first, l_first, l_second)
    is_first = lax.rem(r0, seq) == 0
    is_last = lax.rem(r0 + tm, seq) == 0
    x = x_ref[...]
    x_ext = jnp.concatenate([xp_ref[...], x, xn_ref[...]], axis=0)
    row = lax.broadcasted_iota(I32, (tm, 1), 0)
    kill_prev = jnp.logical_and(row == 0, is_first)
    kill_next = jnp.logical_and(row == tm - 1, is_last)

    def conv_chunk(c):
        lo, hi = c * hy, (c + 1) * hy
        z = jnp.dot(x_ext, w_ref[:, lo:hi], preferred_element_type=F32)
        um1 = jnp.where(kill_prev, 0.0, z[HALO - 1:HALO - 1 + tm])
        up1 = jnp.where(kill_next, 0.0, z[HALO + 1:HALO + 1 + tm])
        return (um1 * cw_ref[0:1, lo:hi] + z[HALO:HALO + tm] * cw_ref[1:2, lo:hi]
                + up1 * cw_ref[2:3, lo:hi] + cb_ref[:, lo:hi])

    x0_ref[...] = conv_chunk(0).astype(BF16)
    vv = (conv_chunk(2) * conv_chunk(1)).astype(BF16)
    zfn = jnp.dot(x, w_ref[:, 3 * hy:3 * hy + fn], preferred_element_type=F32).astype(BF16)

    @pl.when(r0 < t_first)
    def _():
        vva_ref[...] = vv
        zfa_ref[...] = zfn

    @pl.when(r0 >= t_first)
    def _():
        vvb_ref[...] = vv
        zfb_ref[...] = zfn

    g = jnp.dot(x, w_ref[:, 3 * hy + fn:], preferred_element_type=F32)
    gate_ref[...] = jax.nn.sigmoid(g).astype(BF16)


def _proj(hb, w_in, conv_w, conv_b, hy, fn, t_first, l_first, l_second, tm):
    t, d = hb.shape
    n = w_in.shape[1]
    ng = n - 3 * hy - fn
    nh = tm // HALO
    last = t // HALO - 1
    row = lambda c: pl.BlockSpec((tm, c), lambda i: (i, 0))
    na = t_first // tm
    row_a = lambda c: pl.BlockSpec((tm, c), lambda i: (jnp.minimum(i, na - 1), 0))
    row_b = lambda c: pl.BlockSpec((tm, c), lambda i: (jnp.maximum(i - na, 0), 0))
    bf = lambda r, c: jax.ShapeDtypeStruct((r, c), BF16)
    return pl.pallas_call(
        functools.partial(_proj_kernel, tm, t_first, l_first, l_second, hy, fn),
        grid=(t // tm,),
        in_specs=[
            row(d),
            pl.BlockSpec((HALO, d), lambda i: (jnp.maximum(i * nh - 1, 0), 0)),
            pl.BlockSpec((HALO, d), lambda i: (jnp.minimum((i + 1) * nh, last), 0)),
            pl.BlockSpec((d, n), lambda i: (0, 0)),
            pl.BlockSpec((3, 3 * hy), lambda i: (0, 0)),
            pl.BlockSpec((1, 3 * hy), lambda i: (0, 0)),
        ],
        out_specs=[row(hy), row_a(hy), row_b(hy), row_a(fn), row_b(fn), row(ng)],
        out_shape=[bf(t, hy), bf(t_first, hy), bf(t - t_first, hy),
                   bf(t_first, fn), bf(t - t_first, fn), bf(t, ng)],
        compiler_params=_cparams(1),
        name="proj_hyena_pre",
    )(hb, hb, hb, w_in, conv_w, conv_b.reshape(1, 3 * hy))


def _filter_kernel(tr, seq, w1t_ref, w1c_ref, w1s_ref, b1_ref, f1_ref, w2t_ref, b2_ref, f2_ref, w3_ref,
                   dec_ref, o_ref):
    i = pl.program_id(0)

    def tap(n):
        return jnp.where(n < seq, n, 2 * seq - n).astype(F32)

    n_l = i * tr + lax.broadcasted_iota(I32, (1, tr), 1)
    m_l = tap(n_l)
    band = lax.broadcasted_iota(I32, (FILTER_BANDS, 1), 0).astype(F32)
    freqs = 1e-4 + band * ((FILTER_BANDS - 1 - 1e-4) / (FILTER_BANDS - 1))
    fa = freqs * ((2.0 * math.pi / seq) * m_l)
    hi = lax.Precision.HIGHEST
    pre = (w1t_ref[...] * (m_l * (1.0 / (seq - 1)))
           + jnp.dot(w1c_ref[...], jnp.cos(fa), precision=hi, preferred_element_type=F32)
           - jnp.dot(w1s_ref[...], jnp.sin(fa), precision=hi, preferred_element_type=F32))
    h = jnp.sin(f1_ref[...] * (pre + b1_ref[...]))
    h = jnp.sin(f2_ref[...] * (jnp.dot(w2t_ref[...], h, precision=hi, preferred_element_type=F32)
                               + b2_ref[...]))
    out = lax.dot_general(h, w3_ref[...], (((0,), (0,)), ((), ())), precision=hi,
                          preferred_element_type=F32)
    n_s = i * tr + lax.broadcasted_iota(I32, (tr, 1), 0)
    t_s = tap(n_s) * (1.0 / (seq - 1))
    out = out * (jnp.exp(-t_s * jnp.abs(dec_ref[...])) + MOD_SHIFT)
    o_ref[...] = jnp.where(n_s == seq, 0.0, out)


def _filter(seq, w1, b1, f1, w2, b2, f2, w3, dec, tr):
    hid = w1.shape[1]
    hy = w3.shape[1] // 2
    nfwd = seq // tr
    side = lambda i: (0, jnp.where(i >= nfwd, 1, 0))
    full = lambda a: pl.BlockSpec(a.shape, lambda i: (0, 0))
    col = lambda v: v.reshape(hid, 1)
    small = (w1[0:1].T, w1[1:1 + FILTER_BANDS].T, w1[1 + FILTER_BANDS:].T, col(b1), col(f1),
             w2.T, col(b2), col(f2))
    return pl.pallas_call(
        functools.partial(_filter_kernel, tr, seq),
        grid=(2 * seq // tr,),
        in_specs=[full(a) for a in small] + [pl.BlockSpec((hid, hy), side),
                                             pl.BlockSpec((1, hy), side)],
        out_specs=pl.BlockSpec((tr, hy), lambda i: (i, 0)),
        out_shape=jax.ShapeDtypeStruct((2 * seq, hy), F32),
        compiler_params=_cparams(1),
        name="hyena_filter",
    )(*small, w3, dec.reshape(1, 2 * hy))


def _blm_kernel(tj, shared, m_ref, x_ref, o_ref):
    for jj in range(tj):
        m = m_ref[0 if shared else jj]
        o_ref[0, jj] = jnp.dot(m, x_ref[0, jj], preferred_element_type=F32).astype(o_ref.dtype)


def _blm(m, x, tj, out_dtype=BF16):
    g, j, k, c = x.shape
    jm, mr, _ = m.shape
    shared = jm == 1
    tj = min(tj, j)
    m_spec = (pl.BlockSpec((1, mr, k), lambda jb, gb: (0, 0, 0)) if shared
              else pl.BlockSpec((tj, mr, k), lambda jb, gb: (jb, 0, 0)))
    return pl.pallas_call(
        functools.partial(_blm_kernel, tj, shared),
        grid=(j // tj, g),
        in_specs=[m_spec, pl.BlockSpec((1, tj, k, c), lambda jb, gb: (gb, jb, 0, 0))],
        out_specs=pl.BlockSpec((1, tj, mr, c), lambda jb, gb: (gb, jb, 0, 0)),
        out_shape=jax.ShapeDtypeStruct((g, j, mr, c), out_dtype),
        compiler_params=_cparams(2),
        name="dft_stage",
    )(m, x)


def _fnet_s1_kernel(tj, fn, m_ref, r_ref, x_ref, o_ref):
    for jj in range(tj):
        z = jnp.dot(x_ref[0, jj], r_ref[...], preferred_element_type=F32)
        zs = jnp.concatenate([z[:, :fn], z[:, fn:]], axis=0).astype(BF16)
        o_ref[0, jj] = jnp.dot(m_ref[jj], zs, preferred_element_type=F32).astype(o_ref.dtype)


def _fnet_s1(m, r, x, tj):
    g, j, k, fn = x.shape
    _, mr, k2 = m.shape
    tj = min(tj, j)
    return pl.pallas_call(
        functools.partial(_fnet_s1_kernel, tj, fn),
        grid=(j // tj, g),
        in_specs=[pl.BlockSpec((tj, mr, k2), lambda jb, gb: (jb, 0, 0)),
                  pl.BlockSpec(r.shape, lambda jb, gb: (0, 0)),
                  pl.BlockSpec((1, tj, k, fn), lambda jb, gb: (gb, jb, 0, 0))],
        out_specs=pl.BlockSpec((1, tj, mr, fn), lambda jb, gb: (gb, jb, 0, 0)),
        out_shape=jax.ShapeDtypeStruct((g, j, mr, fn), BF16),
        compiler_params=_cparams(2),
        name="fnet_stage1",
    )(m, r, x)


def _hyena_mid_kernel(tk, n2, f_ref, kf_ref, gi_ref, x_ref, o_ref):
    for kk in range(tk):
        a = jnp.dot(f_ref[...], x_ref[0, kk], preferred_element_type=F32)
        ar, ai = a[:n2], a[n2:]
        kr = kf_ref[kk, :n2].astype(F32)
        ki = kf_ref[kk, n2:].astype(F32)
        prod = jnp.concatenate([ar * kr - ai * ki, ar * ki + ai * kr], axis=0).astype(BF16)
        o_ref[0, kk] = jnp.dot(gi_ref[kk], prod, preferred_element_type=F32).astype(o_ref.dtype)


def _hyena_mid(f2s, kf, ginv, x, tk):
    g, n1, r, c = x.shape
    n2 = r // 2
    tk = min(tk, n1)
    return pl.pallas_call(
        functools.partial(_hyena_mid_kernel, tk, n2),
        grid=(n1 // tk, g),
        in_specs=[pl.BlockSpec((r, r), lambda kb, gb: (0, 0)),
                  pl.BlockSpec((tk, r, c), lambda kb, gb: (kb, 0, 0)),
                  pl.BlockSpec((tk, r, r), lambda kb, gb: (kb, 0, 0)),
                  pl.BlockSpec((1, tk, r, c), lambda kb, gb: (gb, kb, 0, 0))],
        out_specs=pl.BlockSpec((1, tk, r, c), lambda kb, gb: (gb, kb, 0, 0)),
        out_shape=jax.ShapeDtypeStruct(x.shape, BF16),
        compiler_params=_cparams(2),
        name="hyena_mid",
    )(f2s, kf, ginv, x)


def _cis(num, den, sign):
    ang = (2.0 * math.pi / den) * lax.rem(num, den).astype(F32)
    return jnp.cos(ang), sign * jnp.sin(ang)


def _stack_complex(mr, mi):
    top = jnp.concatenate([mr, -mi], axis=-1)
    bot = jnp.concatenate([mi, mr], axis=-1)
    return jnp.concatenate([top, bot], axis=-2)


def _iota(shape, axis):
    return lax.broadcasted_iota(I32, shape, axis)


def _hyena_mats(n, n1, n2):
    n1h = n1 // 2
    sh = (n2, n1, n1)
    gr, gi = _cis(_iota(sh, 1) * (n2 * _iota(sh, 2) + _iota(sh, 0)), n, -1.0)
    g_data = _stack_complex(gr[:, :, :n1h], gi[:, :, :n1h]).astype(BF16)
    g_filt = jnp.concatenate([gr, gi], axis=1).astype(BF16)
    sh = (n2, n2)
    fr, fi = _cis(_iota(sh, 0) * _iota(sh, 1), n2, -1.0)
    f2s = _stack_complex(fr, fi).astype(BF16)
    sh = (n1, n2, n2)
    ir, ii = _cis(_iota(sh, 1) * (_iota(sh, 0) + n1 * _iota(sh, 2)), n, 1.0)
    ginv = _stack_complex(ir, ii).astype(BF16)
    sh = (n1h, n1)
    br, bi = _cis(_iota(sh, 0) * _iota(sh, 1), n1, 1.0)
    f1inv = (_stack_complex(br, bi) * (1.0 / n)).astype(BF16)[None]
    return g_data, g_filt, f2s, ginv, f1inv


def _fnet_mats(seq, na, nb, fn):
    gd = fn // FN_GROUPS
    sh = (nb, na, na)
    gr, gi = _cis(_iota(sh, 1) * (nb * _iota(sh, 2) + _iota(sh, 0)), seq, -1.0)
    g1 = _stack_complex(gr, gi).astype(BF16)
    sh = (nb, nb)
    fr, fi = _cis(_iota(sh, 0) * _iota(sh, 1), nb, -1.0)
    scale = 1.0 / math.sqrt(seq * gd)
    f2re = (jnp.concatenate([fr, -fi], axis=-1) * scale).astype(BF16)[None]
    sh = (fn, fn)
    same = (_iota(sh, 0) // gd) == (_iota(sh, 1) // gd)
    cr, ci = _cis(_iota(sh, 0) * _iota(sh, 1), gd, -1.0)
    rmat = jnp.concatenate([jnp.where(same, cr, 0.0), jnp.where(same, ci, 0.0)], axis=1).astype(BF16)
    return g1, f2re, rmat


def _hyena_conv(vv, batch, seq, filt_args, tj):
    c = vv.shape[1]
    n = 2 * seq
    n1 = DFT_RADIX
    n2 = n // n1
    n1h = n1 // 2
    bp = batch // 2
    g_data, g_filt, f2s, ginv, f1inv = _hyena_mats(n, n1, n2)
    kt = _filter(seq, *filt_args, tr=min(512, seq))
    kt = kt.reshape(1, n1, n2, c).transpose(0, 2, 1, 3).astype(BF16)
    ks = _blm(g_filt, kt, tj)
    ks = ks.reshape(1, n2, 2, n1, c).transpose(0, 3, 2, 1, 4).reshape(1, n1, 2 * n2, c)
    kf = _blm(f2s[None], ks, tj, out_dtype=F32)[0]
    x = vv.reshape(bp, 2, n1h, n2, c).transpose(0, 3, 1, 2, 4).reshape(bp, n2, n1, c)
    a = _blm(g_data, x, tj)
    a = a.reshape(bp, n2, 2, n1, c).transpose(0, 3, 2, 1, 4).reshape(bp, n1, 2 * n2, c)
    z = _hyena_mid(f2s, kf, ginv, a, 4)
    z = z.reshape(bp, n1, 2, n2, c).transpose(0, 3, 2, 1, 4).reshape(bp, n2, 2 * n1, c)
    y = _blm(f1inv, z, tj)
    y = y.reshape(bp, n2, 2, n1h, c).transpose(0, 2, 3, 1, 4)
    return y.reshape(batch * seq, c)


def _fnet_mix(u, batch, seq, tj):
    fn = u.shape[1]
    nb = FNET_MINOR
    na = seq // nb
    g1, f2re, rmat = _fnet_mats(seq, na, nb, fn)
    x = u.reshape(batch, na, nb, fn).transpose(0, 2, 1, 3)
    a = _fnet_s1(g1, rmat, x, tj)
    a = a.reshape(batch, nb, 2, na, fn).transpose(0, 3, 2, 1, 4).reshape(batch, na, 2 * nb, fn)
    y = _blm(f2re, a, tj)
    return y.transpose(0, 2, 1, 3).reshape(batch * seq, fn)


def _merge_kernel(alpha, d, n_exp, n_first, conva_ref, convb_ref, vva_ref, vvb_ref, yfna_ref, yfnb_ref,
                  x0_ref, gate_ref, h_ref, bias_ref,
                  why_ref, wfn_ref, wo_ref, g_ref, b_ref, wr_ref, br_ref,
                  h1_ref, idx_ref, wt_ref):
    in_first = pl.program_id(0) < n_first
    pick = lambda a_ref, b_ref: jnp.where(in_first, a_ref[...], b_ref[...])
    vv = pick(vva_ref, vvb_ref).astype(F32)
    y_hy = x0_ref[...].astype(F32) * (pick(conva_ref, convb_ref).astype(F32) + vv * bias_ref[...])
    a = jnp.dot(y_hy.astype(BF16), why_ref[...], preferred_element_type=F32)
    bfn = jnp.dot(pick(yfna_ref, yfnb_ref), wfn_ref[...], preferred_element_type=F32)
    gates = gate_ref[...].astype(F32)
    merged = gates[:, :d] * a + gates[:, d:] * bfn
    m = jnp.dot(merged.astype(BF16), wo_ref[...], preferred_element_type=F32)
    h1 = _layer_norm(alpha * h_ref[...] + m, g_ref[...], b_ref[...])
    _store_token_tiles(h1_ref, h1)
    logits = jnp.dot(h1.astype(BF16), wr_ref[...], preferred_element_type=F32) + br_ref[...]
    lane = lax.broadcasted_iota(I32, logits.shape, 1)
    lane_f = lane.astype(F32)
    neg = jnp.float32(-jnp.inf)
    logits = jnp.where(lane < n_exp, logits, neg)
    vals, idxs = [], []
    for _ in range(TOP_K):
        mx = jnp.max(logits, axis=-1, keepdims=True)
        ix = jnp.min(jnp.where(logits == mx, lane_f, float(LANES)), axis=-1, keepdims=True).astype(I32)
        vals.append(mx)
        idxs.append(ix)
        logits = jnp.where(lane == ix, neg, logits)
    exps = [jnp.exp(v - vals[0]) for v in vals]
    den = exps[0]
    for e in exps[1:]:
        den = den + e
    idx_out = jnp.zeros(lane.shape, I32)
    wt_out = jnp.zeros(lane.shape, F32)
    for k in range(TOP_K):
        idx_out = jnp.where(lane == k, idxs[k], idx_out)
        wt_out = jnp.where(lane == k, exps[k] / den, wt_out)
    idx_ref[...] = idx_out
    wt_ref[...] = wt_out


def _merge(alpha, convs, vvs, yfns, x0, gates, h, bias, why, wfn, wo, g, b, wr, br, tm):
    t, d = h.shape
    hy = x0.shape[1]
    fn = yfns[0].shape[1]
    n_exp = wr.shape[1]
    na = convs[0].shape[0] // tm
    wr_p = jnp.zeros((d, LANES), BF16).at[:, :n_exp].set(wr.astype(BF16))
    br_p = jnp.zeros((1, LANES), F32).at[0, :n_exp].set(br)
    row = lambda c: pl.BlockSpec((tm, c), lambda i: (i, 0))
    row_a = lambda c: pl.BlockSpec((tm, c), lambda i: (jnp.minimum(i, na - 1), 0))
    row_b = lambda c: pl.BlockSpec((tm, c), lambda i: (jnp.maximum(i - na, 0), 0))
    full = lambda a: pl.BlockSpec(a.shape, lambda i: (0,) * a.ndim)
    args = (convs[0], convs[1], vvs[0], vvs[1], yfns[0], yfns[1], x0, gates, h,
            bias.reshape(1, hy), why, wfn, wo, g.reshape(1, d), b.reshape(1, d), wr_p, br_p)
    in_specs = ([row_a(hy), row_b(hy), row_a(hy), row_b(hy), row_a(fn), row_b(fn),
                 row(hy), row(2 * d), row(d)] + [full(a) for a in args[9:]])
    return pl.pallas_call(
        functools.partial(_merge_kernel, alpha, d, n_exp, na),
        grid=(t // tm,),
        in_specs=in_specs,
        out_specs=[pl.BlockSpec((tm * (d // LANES), LANES), lambda i: (i, 0)), row(LANES), row(LANES)],
        out_shape=[jax.ShapeDtypeStruct((t * (d // LANES), LANES), F32),
                   jax.ShapeDtypeStruct((t, LANES), I32), jax.ShapeDtypeStruct((t, LANES), F32)],
        compiler_params=_cparams(1),
        name="merge_route",
    )(*args)


def _rank_kernel(tm, idx_ref, rank_ref, cnt_ref, carry_ref):
    i = pl.program_id(0)

    @pl.when(i == 0)
    def _():
        carry_ref[...] = jnp.zeros_like(carry_ref)

    idx = idx_ref[...]
    lane = lax.broadcasted_iota(I32, idx.shape, 1)
    cols = [idx[:, k:k + 1] for k in range(TOP_K)]
    onehot = jnp.zeros(idx.shape, F32)
    for ck in cols:
        onehot = onehot + (lane == ck).astype(F32)
    r = lax.broadcasted_iota(I32, (tm, tm), 0)
    c = lax.broadcasted_iota(I32, (tm, tm), 1)
    tri = (c < r).astype(BF16)
    before = jnp.dot(tri, onehot.astype(BF16), preferred_element_type=F32) + carry_ref[...]
    out = jnp.zeros(idx.shape, F32)
    for k, ck in enumerate(cols):
        rk = jnp.sum(jnp.where(lane == ck, before, 0.0), axis=-1, keepdims=True)
        out = jnp.where(lane == k, rk, out)
    rank_ref[...] = out.astype(I32)
    total = carry_ref[...] + jnp.sum(onehot, axis=0, keepdims=True)
    carry_ref[...] = total
    cnt_ref[...] = total.astype(I32)


def _ranks(idx128, tm):
    t = idx128.shape[0]
    return pl.pallas_call(
        functools.partial(_rank_kernel, tm),
        grid=(t // tm,),
        in_specs=[pl.BlockSpec((tm, LANES), lambda i: (i, 0))],
        out_specs=[pl.BlockSpec((tm, LANES), lambda i: (i, 0)),
                   pl.BlockSpec((1, LANES), lambda i: (0, 0))],
        out_shape=[jax.ShapeDtypeStruct((t, LANES), I32), jax.ShapeDtypeStruct((1, LANES), I32)],
        scratch_shapes=[pltpu.VMEM((1, LANES), F32)],
        compiler_params=_cparams(1),
        name="expert_rank",
    )(idx128)


def _token_copy(nch, src_ref, src_tok, dst_ref, dst_tok, sem):
    rows = lambda tok: pl.ds(tok * nch if isinstance(tok, int) else pl.multiple_of(tok * nch, nch), nch)
    return pltpu.make_async_copy(src_ref.at[rows(src_tok)], dst_ref.at[rows(dst_tok)], sem)


N_TILE_BUF = 3


def _dispatch_kernel(tm, nch, n_exp, n_blocks, zstart_ref, znum_ref, nused_ref, dest_hbm, h_hbm, xs_hbm,
                     dsm, hbuf, zbuf, sem_idx, sem_tile, sem_row, sem_z):
    i = pl.program_id(0)
    n = pl.num_programs(0)
    slot = lax.rem(i, 2)
    tb = lax.rem(i, N_TILE_BUF)
    tile_rows = tm * nch
    blk_rows = MOE_BLOCK * nch

    def idx_fetch(step, s):
        return pltpu.make_async_copy(dest_hbm.at[step], dsm.at[s], sem_idx.at[s])

    def tile_load(step, b):
        row = pl.multiple_of(step * tile_rows, tile_rows)
        return pltpu.make_async_copy(h_hbm.at[pl.ds(row, tile_rows)], hbuf.at[b], sem_tile.at[b])

    def rows_done(s):
        return pltpu.make_async_copy(hbuf.at[0], xs_hbm.at[pl.ds(0, tile_rows)], sem_row.at[s])

    @pl.when(i == 0)
    def _():
        idx_fetch(0, 0).start()
        tile_load(0, 0).start()

    idx_fetch(i, slot).wait()
    tile_load(i, tb).wait()

    @pl.when(i + 1 < n)
    def _():
        idx_fetch(i + 1, 1 - slot).start()
        tile_load(i + 1, lax.rem(i + 1, N_TILE_BUF)).start()

    for r in range(tm):
        for k in range(TOP_K):
            _token_copy(nch, hbuf.at[tb], r, xs_hbm, dsm[slot, r * TOP_K + k],
                        sem_row.at[slot]).start(priority=k % 2)

    @pl.when(i > 0)
    def _():
        for _ in range(TOP_K):
            rows_done(1 - slot).wait()

    @pl.when(i == n - 1)
    def _():
        for _ in range(TOP_K):
            rows_done(slot).wait()
        zbuf[...] = jnp.zeros_like(zbuf)

        def per_expert(e, carry):
            start = zstart_ref[e]
            num = znum_ref[e]

            def zissue(r, c2):
                _token_copy(nch, zbuf, 0, xs_hbm, start + r, sem_z).start()
                return c2

            def zwait(r, c2):
                _token_copy(nch, zbuf, 0, xs_hbm, 0, sem_z).wait()
                return c2

            lax.fori_loop(0, num, zissue, 0)
            lax.fori_loop(0, num, zwait, 0)
            return carry

        lax.fori_loop(0, n_exp, per_expert, 0)

        def blk_copy(bi):
            row = pl.multiple_of(bi * blk_rows, blk_rows)
            return pltpu.make_async_copy(zbuf, xs_hbm.at[pl.ds(row, blk_rows)], sem_z)

        def bissue(bi, carry):
            blk_copy(bi).start()
            return carry

        def bwait(bi, carry):
            blk_copy(bi).wait()
            return carry

        lax.fori_loop(nused_ref[0], n_blocks, bissue, 0)
        lax.fori_loop(nused_ref[0], n_blocks, bwait, 0)


def _dispatch(zstart, znum, n_used, dest, h1_tt, nch, n_slots, tm):
    t = h1_tt.shape[0] // nch
    n_exp = zstart.shape[0]
    dest2 = dest.reshape(t // tm, tm * TOP_K)
    return pl.pallas_call(
        functools.partial(_dispatch_kernel, tm, nch, n_exp, n_slots // MOE_BLOCK),
        grid_spec=pltpu.PrefetchScalarGridSpec(
            num_scalar_prefetch=3,
            grid=(t // tm,),
            in_specs=[pl.BlockSpec(memory_space=pl.ANY), pl.BlockSpec(memory_space=pl.ANY)],
            out_specs=pl.BlockSpec(memory_space=pl.ANY),
            scratch_shapes=[pltpu.SMEM((2, tm * TOP_K), I32),
                            pltpu.VMEM((N_TILE_BUF, tm * nch, LANES), F32),
                            pltpu.VMEM((MOE_BLOCK * nch, LANES), F32),
                            pltpu.SemaphoreType.DMA((2,)), pltpu.SemaphoreType.DMA((N_TILE_BUF,)),
                            pltpu.SemaphoreType.DMA((2,)), pltpu.SemaphoreType.DMA],
        ),
        out_shape=jax.ShapeDtypeStruct((n_slots * nch, LANES), F32),
        compiler_params=_cparams(1),
        name="moe_dispatch",
    )(zstart, znum, n_used, dest2, h1_tt)


def _expert_kernel(d_ff, nch, be_ref, nused_ref, xs_ref, x_ref, wgu_ref, bgu_ref, wd_ref, bd_ref, y_ref,
                   wgu_q, wd_q, sgu_ref, sd_ref):
    i = pl.program_id(0)
    live = i < nused_ref[0]
    prev = be_ref[jnp.maximum(i - 1, 0)]

    def amax(v):
        return jnp.max(jnp.max(jnp.abs(v), axis=0, keepdims=True), axis=1, keepdims=True)

    def quant(v):
        m = jnp.maximum(amax(v), 1e-30)
        return (v * (FP8_PEAK / m)).astype(FP8), m * (1.0 / FP8_PEAK)

    @pl.when(jnp.logical_and(live, jnp.logical_or(i == 0, be_ref[i] != prev)))
    def _():
        wgu_q[...], sgu_ref[...] = quant(wgu_ref[0])
        wd_q[...], sd_ref[...] = quant(wd_ref[0])

    @pl.when(live)
    def _():
        x = (_load_token_tiles(x_ref, MOE_BLOCK, nch) * xs_ref[0:1, 0:1]).astype(FP8)
        gu = (jnp.dot(x, wgu_q[...], preferred_element_type=F32) * (xs_ref[0:1, 1:2] * sgu_ref[...])
              + bgu_ref[0])
        gu = gu.astype(BF16)
        gate = jnp.minimum(gu[:, :d_ff], SWIGLU_LIMIT)
        up = jnp.clip(gu[:, d_ff:], -SWIGLU_LIMIT, SWIGLU_LIMIT)
        act = gate * jax.nn.sigmoid(SWIGLU_ALPHA * gate) * (up + 1.0)
        a = (act * ACT_SCALE).astype(FP8)
        y = jnp.dot(a, wd_q[...], preferred_element_type=F32) * (sd_ref[...] * (1.0 / ACT_SCALE)) + bd_ref[0]
        _store_token_tiles(y_ref, y)

    @pl.when(i >= nused_ref[0])
    def _():
        y_ref[...] = jnp.zeros_like(y_ref)


def _experts(block_e, n_used, xs_tt, x_bound, layer, wgu_all, bgu, wd_all, bd):
    depth, n_exp, d, two_ff = wgu_all.shape
    x_scale = FP8_PEAK / jnp.maximum(x_bound, 1e-30)
    x_scales = jnp.zeros((1, LANES), F32).at[0, 0].set(x_scale).at[0, 1].set(1.0 / x_scale)
    nch = d // LANES
    n_slots = xs_tt.shape[0] // nch
    d_ff = two_ff // 2
    nb = n_slots // MOE_BLOCK
    live = lambda i, be, nu: jnp.minimum(i, nu[0] - 1)
    wgu = wgu_all.reshape(depth * n_exp, d, two_ff)
    wd = wd_all.reshape(depth * n_exp, d_ff, d)
    w_row = lambda i, be, nu: (layer * n_exp + be[live(i, be, nu)], 0, 0)
    return pl.pallas_call(
        functools.partial(_expert_kernel, d_ff, nch),
        grid_spec=pltpu.PrefetchScalarGridSpec(
            num_scalar_prefetch=2,
            grid=(nb,),
            in_specs=[
                pl.BlockSpec((1, LANES), lambda i, be, nu: (0, 0)),
                pl.BlockSpec((MOE_BLOCK * nch, LANES), lambda i, be, nu: (live(i, be, nu), 0)),
                pl.BlockSpec((1, d, two_ff), w_row),
                pl.BlockSpec((1, 1, two_ff), lambda i, be, nu: (be[live(i, be, nu)], 0, 0)),
                pl.BlockSpec((1, d_ff, d), w_row),
                pl.BlockSpec((1, 1, d), lambda i, be, nu: (be[live(i, be, nu)], 0, 0)),
            ],
            out_specs=pl.BlockSpec((MOE_BLOCK * nch, LANES), lambda i, be, nu: (i, 0)),
            scratch_shapes=[pltpu.VMEM((d, two_ff), FP8), pltpu.VMEM((d_ff, d), FP8),
                            pltpu.VMEM((1, 1), F32), pltpu.VMEM((1, 1), F32)],
        ),
        out_shape=jax.ShapeDtypeStruct((n_slots * nch, LANES), F32),
        compiler_params=_cparams(1, VMEM_LIMIT_EXPERTS),
        name="moe_experts",
    )(block_e, n_used, x_scales, xs_tt, wgu, bgu.reshape(n_exp, 1, two_ff), wd,
      bd.reshape(n_exp, 1, d))


def _combine_kernel(tm, nch, alpha, step0, with_bf16, dest_hbm, h1_ref, wt_ref, g_ref, b_ref, y_hbm,
                    *rest):
    if with_bf16:
        o_ref, ob16_ref, dsm, ybuf, sem_idx, sem = rest
    else:
        o_ref, dsm, ybuf, sem_idx, sem = rest
    i = pl.program_id(0)
    n = pl.num_programs(0)
    slot = lax.rem(i, 2)

    def idx_fetch(step, s):
        return pltpu.make_async_copy(dest_hbm.at[step0 + step], dsm.at[s], sem_idx.at[s])

    def gather(s):
        for r in range(tm):
            for k in range(TOP_K):
                _token_copy(nch, y_hbm, dsm[s, r * TOP_K + k], ybuf.at[s, k], r,
                            sem.at[s]).start(priority=k % 2)

    @pl.when(i == 0)
    def _():
        first = idx_fetch(0, 0)
        first.start()
        first.wait()
        gather(0)
        idx_fetch(1, 1).start()

    @pl.when(i + 1 < n)
    def _():
        idx_fetch(i + 1, 1 - slot).wait()
        gather(1 - slot)

    @pl.when(i + 2 < n)
    def _():
        idx_fetch(i + 2, slot).start()

    for k in range(TOP_K):
        pltpu.make_async_copy(y_hbm.at[pl.ds(0, tm * nch)], ybuf.at[slot, k], sem.at[slot]).wait()
    wt = wt_ref[...]
    f = wt[:, 0:1] * _load_token_tiles(ybuf.at[slot, 0], tm, nch)
    for k in range(1, TOP_K):
        f = f + wt[:, k:k + 1] * _load_token_tiles(ybuf.at[slot, k], tm, nch)
    h1 = _load_token_tiles(h1_ref, tm, nch)
    out = _layer_norm(alpha * h1 + f, g_ref[...], b_ref[...])
    o_ref[...] = out
    if with_bf16:
        ob16_ref[...] = out.astype(BF16)


def _combine(alpha, dest, h1_tt, wt128, g, b, y_tt, tm, tok0, ntok, with_bf16):
    d = g.shape[0]
    nch = d // LANES
    t = h1_tt.shape[0] // nch
    dest2 = dest.reshape(t // tm, tm * TOP_K)
    step0 = tok0 // tm
    nsteps = ntok // tm
    assert nsteps >= 2 and tok0 % tm == 0 and ntok % tm == 0
    vec = pl.BlockSpec((1, d), lambda i: (0, 0))
    out_specs = [pl.BlockSpec((tm, d), lambda i: (i, 0))]
    out_shape = [jax.ShapeDtypeStruct((ntok, d), F32)]
    if with_bf16:
        out_specs.append(pl.BlockSpec((tm, d), lambda i: (i, 0)))
        out_shape.append(jax.ShapeDtypeStruct((ntok, d), BF16))
    return pl.pallas_call(
        functools.partial(_combine_kernel, tm, nch, alpha, step0, with_bf16),
        grid=(nsteps,),
        in_specs=[pl.BlockSpec(memory_space=pl.ANY),
                  pl.BlockSpec((tm * nch, LANES), lambda i: (step0 + i, 0)),
                  pl.BlockSpec((tm, LANES), lambda i: (step0 + i, 0)), vec, vec,
                  pl.BlockSpec(memory_space=pl.ANY)],
        out_specs=out_specs,
        out_shape=out_shape,
        scratch_shapes=[pltpu.SMEM((2, tm * TOP_K), I32),
                        pltpu.VMEM((2, TOP_K, tm * nch, LANES), F32),
                        pltpu.SemaphoreType.DMA((2,)), pltpu.SemaphoreType.DMA((2,))],
        compiler_params=_cparams(1),
        name="moe_combine",
    )(dest2, h1_tt, wt128, g.reshape(1, d), b.reshape(1, d), y_tt)


def _route_plan(idx, rank, counts, n_blocks):
    padded = (counts + MOE_BLOCK - 1) // MOE_BLOCK * MOE_BLOCK
    pad_ends = jnp.cumsum(padded)
    pad_starts = pad_ends - padded
    dest = pad_starts[idx] + rank
    block_start = jnp.arange(n_blocks, dtype=I32) * MOE_BLOCK
    n_exp = counts.shape[0]
    block_e = jnp.minimum(jnp.sum(block_start[:, None] >= pad_ends[None, :], axis=1), n_exp - 1).astype(I32)
    n_used = (pad_ends[-1] // MOE_BLOCK).astype(I32).reshape(1)
    return dest.astype(I32), block_e, n_used, (pad_starts + counts).astype(I32), (padded - counts).astype(I32)


def _tile(n, pref):
    t = min(pref, n)
    while n % t:
        t //= 2
    return t


def kernel(x_prompt, x_sample, ln_in_g, ln_in_b, w_in, conv_w, conv_b, filt_w1, filt_b1, filt_freq1, filt_w2, filt_b2, filt_freq2, filt_w3, filt_decay, hy_bias, w_hy_out, w_fn_out, w_o, ln1_g, ln1_b, w_router, b_router, w_gu, b_gu, w_down, b_down, ln2_g, ln2_b):
    bp, lp, d = x_prompt.shape
    bs, ls, _ = x_sample.shape
    depth = w_in.shape[0]
    hy = hy_bias.shape[-1]
    fn = w_fn_out.shape[1]
    n_exp = w_router.shape[-1]
    tp, ts = bp * lp, bs * ls
    t = tp + ts
    alpha = (2 * depth) ** 0.25
    trunks = ((bp, lp), (bs, ls))
    n_blocks = -(-(t * TOP_K) // MOE_BLOCK) + n_exp
    n_slots = n_blocks * MOE_BLOCK

    tm_ln = _tile(math.gcd(tp, ts), 1024)
    tm_tok = _tile(math.gcd(lp, ls), 512)
    tm_proj = _tile(math.gcd(lp, ls), 1024)
    tm_row = _tile(math.gcd(tp, ts), 256)

    h, hb = _ln_in(x_prompt.reshape(tp, d), x_sample.reshape(ts, d), ln_in_g, ln_in_b, tm_ln)
    out = None
    for l in range(depth):
        x0, vv_a, vv_b, zf_a, zf_b, gates = _proj(hb, w_in[l].astype(BF16), conv_w[l], conv_b[l],
                                                  hy, fn, tp, lp, ls, tm_proj)
        filt_args = (filt_w1[l], filt_b1[l], filt_freq1[l], filt_w2[l], filt_b2[l], filt_freq2[l],
                     filt_w3[l], filt_decay[l])
        vvs, zfs = (vv_a, vv_b), (zf_a, zf_b)
        convs = [_hyena_conv(v, b, s, filt_args, 8) for v, (b, s) in zip(vvs, trunks)]
        yfns = [_fnet_mix(z, b, s, 8) for z, (b, s) in zip(zfs, trunks)]
        h1, idx128, wt128 = _merge(alpha, convs, vvs, yfns, x0, gates, h, hy_bias[l],
                                   w_hy_out[l].astype(BF16), w_fn_out[l].astype(BF16),
                                   w_o[l].astype(BF16), ln1_g[l], ln1_b[l], w_router[l], b_router[l],
                                   tm_tok)
        rank128, cnt128 = _ranks(idx128, tm_tok)
        dest, block_e, n_used, zstart, znum = _route_plan(
            idx128[:, :TOP_K], rank128[:, :TOP_K], cnt128[0, :n_exp], n_blocks)
        xs = _dispatch(zstart, znum, n_used, dest, h1, d // LANES, n_slots, tm_row)
        x_bound = math.sqrt(d) * jnp.max(jnp.abs(ln1_g[l])) + jnp.max(jnp.abs(ln1_b[l]))
        y = _experts(block_e, n_used, xs, x_bound, l, w_gu, b_gu[l], w_down, b_down[l])
        comb = functools.partial(_combine, alpha, dest, h1, wt128, ln2_g[l], ln2_b[l], y, tm_row)
        if l + 1 < depth:
            h, hb = comb(0, t, True)
        else:
            out = (comb(0, tp, False)[0], comb(tp, ts, False)[0])
    return (out[0].reshape(bp, lp, d), out[1].reshape(bs, ls, d))
```

```python
import functools
import math

import jax
import jax.numpy as jnp
from jax import lax
from jax.experimental import pallas as pl
from jax.experimental.pallas import tpu as pltpu

F32 = jnp.float32
BF16 = jnp.bfloat16
I32 = jnp.int32
FP8 = jnp.float8_e4m3fn
FP8_PEAK = 256.0

TOP_K = 4
FN_GROUPS = 4
MOE_BLOCK = 512
FILTER_BANDS = 16
MOD_SHIFT = 0.05
SWIGLU_LIMIT = 7.0
SWIGLU_ALPHA = 1.702
ACT_SCALE = 4.0
LN_EPS = 1e-5
LANES = 128
SUB = 8
DFT_RADIX = 128
FNET_MINOR = 64
VMEM_LIMIT = 48 * 1024 * 1024
VMEM_LIMIT_EXPERTS = 56 * 1024 * 1024


def _cparams(ndim, vmem=VMEM_LIMIT):
    return pltpu.CompilerParams(dimension_semantics=("arbitrary",) * ndim,
                                vmem_limit_bytes=vmem)


def _store_token_tiles(ref, x):
    rows, d = x.shape
    nch = d // LANES
    for c in range(nch):
        ref[pl.ds(c, rows, stride=nch), :] = x[:, c * LANES:(c + 1) * LANES]


def _load_token_tiles(ref, rows, nch):
    return jnp.concatenate([ref[pl.ds(c, rows, stride=nch), :] for c in range(nch)], axis=1)


def _layer_norm(x, g, b):
    mu = jnp.mean(x, axis=-1, keepdims=True)
    xc = x - mu
    var = jnp.mean(xc * xc, axis=-1, keepdims=True)
    return xc * lax.rsqrt(var + LN_EPS) * g + b


def _ln_in_kernel(n_first, xa_ref, xb_ref, g_ref, b_ref, h_ref, hb_ref):
    i = pl.program_id(0)

    def emit(x_ref):
        y = _layer_norm(x_ref[...], g_ref[...], b_ref[...])
        h_ref[...] = y
        hb_ref[...] = y.astype(BF16)

    @pl.when(i < n_first)
    def _():
        emit(xa_ref)

    @pl.when(i >= n_first)
    def _():
        emit(xb_ref)


def _ln_in(xa, xb, g, b, tm):
    ta, d = xa.shape
    tb = xb.shape[0]
    na, nb = ta // tm, tb // tm
    t = ta + tb
    return pl.pallas_call(
        functools.partial(_ln_in_kernel, na),
        grid=(na + nb,),
        in_specs=[
            pl.BlockSpec((tm, d), lambda i: (jnp.minimum(i, na - 1), 0)),
            pl.BlockSpec((tm, d), lambda i: (jnp.maximum(i - na, 0), 0)),
            pl.BlockSpec((1, d), lambda i: (0, 0)),
            pl.BlockSpec((1, d), lambda i: (0, 0)),
        ],
        out_specs=[pl.BlockSpec((tm, d), lambda i: (i, 0)),
                   pl.BlockSpec((tm, d), lambda i: (i, 0))],
        out_shape=[jax.ShapeDtypeStruct((t, d), F32), jax.ShapeDtypeStruct((t, d), BF16)],
        compiler_params=_cparams(1),
        name="ln_in",
    )(xa, xb, g.reshape(1, d), b.reshape(1, d))


HALO = 16


def _proj_kernel(tm, t_first, l_first, l_second, hy, fn, x_ref, xp_ref, xn_ref, w_ref, cw_ref, cb_ref,
                 x0_ref, vva_ref, vvb_ref, zfa_ref, zfb_ref, gate_ref):
    i = pl.program_id(0)
    r0 = i * tm
    seq = jnp.where(r0 < t_first, l_first, l_second)
    is_first = lax.rem(r0, seq) == 0
    is_last = lax.rem(r0 + tm, seq) == 0
    x = x_ref[...]
    x_ext = jnp.concatenate([xp_ref[...], x, xn_ref[...]], axis=0)
    row = lax.broadcasted_iota(I32, (tm, 1), 0)
    kill_prev = jnp.logical_and(row == 0, is_first)
    kill_next = jnp.logical_and(row == tm - 1, is_last)

    def conv_chunk(c):
        lo, hi = c * hy, (c + 1) * hy
        z = jnp.dot(x_ext, w_ref[:, lo:hi], preferred_element_type=F32)
        um1 = jnp.where(kill_prev, 0.0, z[HALO - 1:HALO - 1 + tm])
        up1 = jnp.where(kill_next, 0.0, z[HALO + 1:HALO + 1 + tm])
        return (um1 * cw_ref[0:1, lo:hi] + z[HALO:HALO + tm] * cw_ref[1:2, lo:hi]
                + up1 * cw_ref[2:3, lo:hi] + cb_ref[:, lo:hi])

    x0_ref[...] = conv_chunk(0).astype(BF16)
    vv = (conv_chunk(2) * conv_chunk(1)).astype(BF16)
    zfn = jnp.dot(x, w_ref[:, 3 * hy:3 * hy + fn], preferred_element_type=F32).astype(BF16)

    @pl.when(r0 < t_first)
    def _():
        vva_ref[...] = vv
        zfa_ref[...] = zfn

    @pl.when(r0 >= t_first)
    def _():
        vvb_ref[...] = vv
        zfb_ref[...] = zfn

    g = jnp.dot(x, w_ref[:, 3 * hy + fn:], preferred_element_type=F32)
    gate_ref[...] = jax.nn.sigmoid(g).astype(BF16)


def _proj(hb, w_in, conv_w, conv_b, hy, fn, t_first, l_first, l_second, tm):
    t, d = hb.shape
    n = w_in.shape[1]
    ng = n - 3 * hy - fn
    nh = tm // HALO
    last = t // HALO - 1
    row = lambda c: pl.BlockSpec((tm, c), lambda i: (i, 0))
    na = t_first // tm
    row_a = lambda c: pl.BlockSpec((tm, c), lambda i: (jnp.minimum(i, na - 1), 0))
    row_b = lambda c: pl.BlockSpec((tm, c), lambda i: (jnp.maximum(i - na, 0), 0))
    bf = lambda r, c: jax.ShapeDtypeStruct((r, c), BF16)
    return pl.pallas_call(
        functools.partial(_proj_kernel, tm, t_first, l_first, l_second, hy, fn),
        grid=(t // tm,),
        in_specs=[
            row(d),
            pl.BlockSpec((HALO, d), lambda i: (jnp.maximum(i * nh - 1, 0), 0)),
            pl.BlockSpec((HALO, d), lambda i: (jnp.minimum((i + 1) * nh, last), 0)),
            pl.BlockSpec((d, n), lambda i: (0, 0)),
            pl.BlockSpec((3, 3 * hy), lambda i: (0, 0)),
            pl.BlockSpec((1, 3 * hy), lambda i: (0, 0)),
        ],
        out_specs=[row(hy), row_a(hy), row_b(hy), row_a(fn), row_b(fn), row(ng)],
        out_shape=[bf(t, hy), bf(t_first, hy), bf(t - t_first, hy),
                   bf(t_first, fn), bf(t - t_first, fn), bf(t, ng)],
        compiler_params=_cparams(1),
        name="proj_hyena_pre",
    )(hb, hb, hb, w_in, conv_w, conv_b.reshape(1, 3 * hy))


def _filter_kernel(tr, seq, w1t_ref, w1c_ref, w1s_ref, b1_ref, f1_ref, w2t_ref, b2_ref, f2_ref, w3_ref,
                   dec_ref, o_ref):
    i = pl.program_id(0)

    def tap(n):
        return jnp.where(n < seq, n, 2 * seq - n).astype(F32)

    n_l = i * tr + lax.broadcasted_iota(I32, (1, tr), 1)
    m_l = tap(n_l)
    band = lax.broadcasted_iota(I32, (FILTER_BANDS, 1), 0).astype(F32)
    freqs = 1e-4 + band * ((FILTER_BANDS - 1 - 1e-4) / (FILTER_BANDS - 1))
    fa = freqs * ((2.0 * math.pi / seq) * m_l)
    hi = lax.Precision.HIGHEST
    pre = (w1t_ref[...] * (m_l * (1.0 / (seq - 1)))
           + jnp.dot(w1c_ref[...], jnp.cos(fa), precision=hi, preferred_element_type=F32)
           - jnp.dot(w1s_ref[...], jnp.sin(fa), precision=hi, preferred_element_type=F32))
    h = jnp.sin(f1_ref[...] * (pre + b1_ref[...]))
    h = jnp.sin(f2_ref[...] * (jnp.dot(w2t_ref[...], h, precision=hi, preferred_element_type=F32)
                               + b2_ref[...]))
    out = lax.dot_general(h, w3_ref[...], (((0,), (0,)), ((), ())), precision=hi,
                          preferred_element_type=F32)
    n_s = i * tr + lax.broadcasted_iota(I32, (tr, 1), 0)
    t_s = tap(n_s) * (1.0 / (seq - 1))
    out = out * (jnp.exp(-t_s * jnp.abs(dec_ref[...])) + MOD_SHIFT)
    o_ref[...] = jnp.where(n_s == seq, 0.0, out)


def _filter(seq, w1, b1, f1, w2, b2, f2, w3, dec, tr):
    hid = w1.shape[1]
    hy = w3.shape[1] // 2
    nfwd = seq // tr
    side = lambda i: (0, jnp.where(i >= nfwd, 1, 0))
    full = lambda a: pl.BlockSpec(a.shape, lambda i: (0, 0))
    col = lambda v: v.reshape(hid, 1)
    small = (w1[0:1].T, w1[1:1 + FILTER_BANDS].T, w1[1 + FILTER_BANDS:].T, col(b1), col(f1),
             w2.T, col(b2), col(f2))
    return pl.pallas_call(
        functools.partial(_filter_kernel, tr, seq),
        grid=(2 * seq // tr,),
        in_specs=[full(a) for a in small] + [pl.BlockSpec((hid, hy), side),
                                             pl.BlockSpec((1, hy), side)],
        out_specs=pl.BlockSpec((tr, hy), lambda i: (i, 0)),
        out_shape=jax.ShapeDtypeStruct((2 * seq, hy), F32),
        compiler_params=_cparams(1),
        name="hyena_filter",
    )(*small, w3, dec.reshape(1, 2 * hy))


def _blm_kernel(tj, shared, m_ref, x_ref, o_ref):
    for jj in range(tj):
        m = m_ref[0 if shared else jj]
        o_ref[0, jj] = jnp.dot(m, x_ref[0, jj], preferred_element_type=F32).astype(o_ref.dtype)


def _blm(m, x, tj, out_dtype=BF16):
    g, j, k, c = x.shape
    jm, mr, _ = m.shape
    shared = jm == 1
    tj = min(tj, j)
    m_spec = (pl.BlockSpec((1, mr, k), lambda jb, gb: (0, 0, 0)) if shared
              else pl.BlockSpec((tj, mr, k), lambda jb, gb: (jb, 0, 0)))
    return pl.pallas_call(
        functools.partial(_blm_kernel, tj, shared),
        grid=(j // tj, g),
        in_specs=[m_spec, pl.BlockSpec((1, tj, k, c), lambda jb, gb: (gb, jb, 0, 0))],
        out_specs=pl.BlockSpec((1, tj, mr, c), lambda jb, gb: (gb, jb, 0, 0)),
        out_shape=jax.ShapeDtypeStruct((g, j, mr, c), out_dtype),
        compiler_params=_cparams(2),
        name="dft_stage",
    )(m, x)


def _fnet_s1_kernel(tj, fn, m_ref, r_ref, x_ref, o_ref):
    for jj in range(tj):
        z = jnp.dot(x_ref[0, jj], r_ref[...], preferred_element_type=F32)
        zs = jnp.concatenate([z[:, :fn], z[:, fn:]], axis=0).astype(BF16)
        o_ref[0, jj] = jnp.dot(m_ref[jj], zs, preferred_element_type=F32).astype(o_ref.dtype)


def _fnet_s1(m, r, x, tj):
    g, j, k, fn = x.shape
    _, mr, k2 = m.shape
    tj = min(tj, j)
    return pl.pallas_call(
        functools.partial(_fnet_s1_kernel, tj, fn),
        grid=(j // tj, g),
        in_specs=[pl.BlockSpec((tj, mr, k2), lambda jb, gb: (jb, 0, 0)),
                  pl.BlockSpec(r.shape, lambda jb, gb: (0, 0)),
                  pl.BlockSpec((1, tj, k, fn), lambda jb, gb: (gb, jb, 0, 0))],
        out_specs=pl.BlockSpec((1, tj, mr, fn), lambda jb, gb: (gb, jb, 0, 0)),
        out_shape=jax.ShapeDtypeStruct((g, j, mr, fn), BF16),
        compiler_params=_cparams(2),
        name="fnet_stage1",
    )(m, r, x)


def _hyena_mid_kernel(tk, n2, f_ref, kf_ref, gi_ref, x_ref, o_ref):
    for kk in range(tk):
        a = jnp.dot(f_ref[...], x_ref[0, kk], preferred_element_type=F32)
        ar, ai = a[:n2], a[n2:]
        kr = kf_ref[kk, :n2].astype(F32)
        ki = kf_ref[kk, n2:].astype(F32)
        prod = jnp.concatenate([ar * kr - ai * ki, ar * ki + ai * kr], axis=0).astype(BF16)
        o_ref[0, kk] = jnp.dot(gi_ref[kk], prod, preferred_element_type=F32).astype(o_ref.dtype)


def _hyena_mid(f2s, kf, ginv, x, tk):
    g, n1, r, c = x.shape
    n2 = r // 2
    tk = min(tk, n1)
    return pl.pallas_call(
        functools.partial(_hyena_mid_kernel, tk, n2),
        grid=(n1 // tk, g),
        in_specs=[pl.BlockSpec((r, r), lambda kb, gb: (0, 0)),
                  pl.BlockSpec((tk, r, c), lambda kb, gb: (kb, 0, 0)),
                  pl.BlockSpec((tk, r, r), lambda kb, gb: (kb, 0, 0)),
                  pl.BlockSpec((1, tk, r, c), lambda kb, gb: (gb, kb, 0, 0))],
        out_specs=pl.BlockSpec((1, tk, r, c), lambda kb, gb: (gb, kb, 0, 0)),
        out_shape=jax.ShapeDtypeStruct(x.shape, BF16),
        compiler_params=_cparams(2),
        name="hyena_mid",
    )(f2s, kf, ginv, x)


def _cis(num, den, sign):
    ang = (2.0 * math.pi / den) * lax.rem(num, den).astype(F32)
    return jnp.cos(ang), sign * jnp.sin(ang)


def _stack_complex(mr, mi):
    top = jnp.concatenate([mr, -mi], axis=-1)
    bot = jnp.concatenate([mi, mr], axis=-1)
    return jnp.concatenate([top, bot], axis=-2)


def _iota(shape, axis):
    return lax.broadcasted_iota(I32, shape, axis)


def _hyena_mats(n, n1, n2):
    n1h = n1 // 2
    sh = (n2, n1, n1)
    gr, gi = _cis(_iota(sh, 1) * (n2 * _iota(sh, 2) + _iota(sh, 0)), n, -1.0)
    g_data = _stack_complex(gr[:, :, :n1h], gi[:, :, :n1h]).astype(BF16)
    g_filt = jnp.concatenate([gr, gi], axis=1).astype(BF16)
    sh = (n2, n2)
    fr, fi = _cis(_iota(sh, 0) * _iota(sh, 1), n2, -1.0)
    f2s = _stack_complex(fr, fi).astype(BF16)
    sh = (n1, n2, n2)
    ir, ii = _cis(_iota(sh, 1) * (_iota(sh, 0) + n1 * _iota(sh, 2)), n, 1.0)
    ginv = _stack_complex(ir, ii).astype(BF16)
    sh = (n1h, n1)
    br, bi = _cis(_iota(sh, 0) * _iota(sh, 1), n1, 1.0)
    f1inv = (_stack_complex(br, bi) * (1.0 / n)).astype(BF16)[None]
    return g_data, g_filt, f2s, ginv, f1inv


def _fnet_mats(seq, na, nb, fn):
    gd = fn // FN_GROUPS
    sh = (nb, na, na)
    gr, gi = _cis(_iota(sh, 1) * (nb * _iota(sh, 2) + _iota(sh, 0)), seq, -1.0)
    g1 = _stack_complex(gr, gi).astype(BF16)
    sh = (nb, nb)
    fr, fi = _cis(_iota(sh, 0) * _iota(sh, 1), nb, -1.0)
    scale = 1.0 / math.sqrt(seq * gd)
    f2re = (jnp.concatenate([fr, -fi], axis=-1) * scale).astype(BF16)[None]
    sh = (fn, fn)
    same = (_iota(sh, 0) // gd) == (_iota(sh, 1) // gd)
    cr, ci = _cis(_iota(sh, 0) * _iota(sh, 1), gd, -1.0)
    rmat = jnp.concatenate([jnp.where(same, cr, 0.0), jnp.where(same, ci, 0.0)], axis=1).astype(BF16)
    return g1, f2re, rmat


def _hyena_conv(vv, batch, seq, filt_args, tj):
    c = vv.shape[1]
    n = 2 * seq
    n1 = DFT_RADIX
    n2 = n // n1
    n1h = n1 // 2
    bp = batch // 2
    g_data, g_filt, f2s, ginv, f1inv = _hyena_mats(n, n1, n2)
    kt = _filter(seq, *filt_args, tr=min(512, seq))
    kt = kt.reshape(1, n1, n2, c).transpose(0, 2, 1, 3).astype(BF16)
    ks = _blm(g_filt, kt, tj)
    ks = ks.reshape(1, n2, 2, n1, c).transpose(0, 3, 2, 1, 4).reshape(1, n1, 2 * n2, c)
    kf = _blm(f2s[None], ks, tj)[0]
    x = vv.reshape(bp, 2, n1h, n2, c).transpose(0, 3, 1, 2, 4).reshape(bp, n2, n1, c)
    a = _blm(g_data, x, tj)
    a = a.reshape(bp, n2, 2, n1, c).transpose(0, 3, 2, 1, 4).reshape(bp, n1, 2 * n2, c)
    z = _hyena_mid(f2s, kf, ginv, a, tj // 2)
    z = z.reshape(bp, n1, 2, n2, c).transpose(0, 3, 2, 1, 4).reshape(bp, n2, 2 * n1, c)
    y = _blm(f1inv, z, tj)
    y = y.reshape(bp, n2, 2, n1h, c).transpose(0, 2, 3, 1, 4)
    return y.reshape(batch * seq, c)


def _fnet_mix(u, batch, seq, tj):
    fn = u.shape[1]
    nb = FNET_MINOR
    na = seq // nb
    g1, f2re, rmat = _fnet_mats(seq, na, nb, fn)
    x = u.reshape(batch, na, nb, fn).transpose(0, 2, 1, 3)
    a = _fnet_s1(g1, rmat, x, tj)
    a = a.reshape(batch, nb, 2, na, fn).transpose(0, 3, 2, 1, 4).reshape(batch, na, 2 * nb, fn)
    y = _blm(f2re, a, tj)
    return y.transpose(0, 2, 1, 3).reshape(batch * seq, fn)


def _merge_kernel(alpha, d, n_exp, n_first, conva_ref, convb_ref, vva_ref, vvb_ref, yfna_ref, yfnb_ref,
                  x0_ref, gate_ref, h_ref, bias_ref,
                  why_ref, wfn_ref, wo_ref, g_ref, b_ref, wr_ref, br_ref,
                  h1_ref, idx_ref, wt_ref, rank_ref, cnt_ref, carry_ref):
    in_first = pl.program_id(0) < n_first
    pick = lambda a_ref, b_ref: jnp.where(in_first, a_ref[...], b_ref[...])
    vv = pick(vva_ref, vvb_ref).astype(F32)
    y_hy = x0_ref[...].astype(F32) * (pick(conva_ref, convb_ref).astype(F32) + vv * bias_ref[...])
    a = jnp.dot(y_hy.astype(BF16), why_ref[...], preferred_element_type=F32)
    bfn = jnp.dot(pick(yfna_ref, yfnb_ref), wfn_ref[...], preferred_element_type=F32)
    gates = gate_ref[...].astype(F32)
    merged = gates[:, :d] * a + gates[:, d:] * bfn
    m = jnp.dot(merged.astype(BF16), wo_ref[...], preferred_element_type=F32)
    h1 = _layer_norm(alpha * h_ref[...] + m, g_ref[...], b_ref[...])
    _store_token_tiles(h1_ref, h1)
    logits = jnp.dot(h1.astype(BF16), wr_ref[...], preferred_element_type=F32) + br_ref[...]
    lane = lax.broadcasted_iota(I32, logits.shape, 1)
    lane_f = lane.astype(F32)
    neg = jnp.float32(-jnp.inf)
    logits = jnp.where(lane < n_exp, logits, neg)
    vals, idxs = [], []
    for _ in range(TOP_K):
        mx = jnp.max(logits, axis=-1, keepdims=True)
        ix = jnp.min(jnp.where(logits == mx, lane_f, float(LANES)), axis=-1, keepdims=True).astype(I32)
        vals.append(mx)
        idxs.append(ix)
        logits = jnp.where(lane == ix, neg, logits)
    exps = [jnp.exp(v - vals[0]) for v in vals]
    den = exps[0]
    for e in exps[1:]:
        den = den + e
    @pl.when(pl.program_id(0) == 0)
    def _():
        carry_ref[...] = jnp.zeros_like(carry_ref)

    tm = lane.shape[0]
    onehot = jnp.zeros(lane.shape, F32)
    for ix in idxs:
        onehot = onehot + (lane == ix).astype(F32)
    tri = (lax.broadcasted_iota(I32, (tm, tm), 1) < lax.broadcasted_iota(I32, (tm, tm), 0)).astype(BF16)
    before = jnp.dot(tri, onehot.astype(BF16), preferred_element_type=F32) + carry_ref[...]
    idx_out = jnp.zeros(lane.shape, I32)
    wt_out = jnp.zeros(lane.shape, F32)
    rank_out = jnp.zeros(lane.shape, F32)
    for k in range(TOP_K):
        idx_out = jnp.where(lane == k, idxs[k], idx_out)
        wt_out = jnp.where(lane == k, exps[k] / den, wt_out)
        rk = jnp.sum(jnp.where(lane == idxs[k], before, 0.0), axis=-1, keepdims=True)
        rank_out = jnp.where(lane == k, rk, rank_out)
    idx_ref[...] = idx_out
    wt_ref[...] = wt_out
    rank_ref[...] = rank_out.astype(I32)
    total = carry_ref[...] + jnp.sum(onehot, axis=0, keepdims=True)
    carry_ref[...] = total
    cnt_ref[...] = total.astype(I32)


def _merge(alpha, convs, vvs, yfns, x0, gates, h, bias, why, wfn, wo, g, b, wr, br, tm):
    t, d = h.shape
    hy = x0.shape[1]
    fn = yfns[0].shape[1]
    n_exp = wr.shape[1]
    na = convs[0].shape[0] // tm
    wr_p = jnp.zeros((d, LANES), BF16).at[:, :n_exp].set(wr.astype(BF16))
    br_p = jnp.zeros((1, LANES), F32).at[0, :n_exp].set(br)
    row = lambda c: pl.BlockSpec((tm, c), lambda i: (i, 0))
    row_a = lambda c: pl.BlockSpec((tm, c), lambda i: (jnp.minimum(i, na - 1), 0))
    row_b = lambda c: pl.BlockSpec((tm, c), lambda i: (jnp.maximum(i - na, 0), 0))
    full = lambda a: pl.BlockSpec(a.shape, lambda i: (0,) * a.ndim)
    args = (convs[0], convs[1], vvs[0], vvs[1], yfns[0], yfns[1], x0, gates, h,
            bias.reshape(1, hy), why, wfn, wo, g.reshape(1, d), b.reshape(1, d), wr_p, br_p)
    in_specs = ([row_a(hy), row_b(hy), row_a(hy), row_b(hy), row_a(fn), row_b(fn),
                 row(hy), row(2 * d), row(d)] + [full(a) for a in args[9:]])
    return pl.pallas_call(
        functools.partial(_merge_kernel, alpha, d, n_exp, na),
        grid=(t // tm,),
        in_specs=in_specs,
        out_specs=[pl.BlockSpec((tm * (d // LANES), LANES), lambda i: (i, 0)), row(LANES), row(LANES),
                   row(LANES), pl.BlockSpec((1, LANES), lambda i: (0, 0))],
        out_shape=[jax.ShapeDtypeStruct((t * (d // LANES), LANES), F32),
                   jax.ShapeDtypeStruct((t, LANES), I32), jax.ShapeDtypeStruct((t, LANES), F32),
                   jax.ShapeDtypeStruct((t, LANES), I32), jax.ShapeDtypeStruct((1, LANES), I32)],
        scratch_shapes=[pltpu.VMEM((1, LANES), F32)],
        compiler_params=_cparams(1),
        name="merge_route",
    )(*args)


def _token_copy(nch, src_ref, src_tok, dst_ref, dst_tok, sem):
    rows = lambda tok: pl.ds(tok * nch if isinstance(tok, int) else pl.multiple_of(tok * nch, nch), nch)
    return pltpu.make_async_copy(src_ref.at[rows(src_tok)], dst_ref.at[rows(dst_tok)], sem)


N_TILE_BUF = 3


def _dispatch_kernel(tm, nch, n_exp, n_blocks, zstart_ref, znum_ref, nused_ref, dest_hbm, h_hbm, xs_hbm,
                     dsm, hbuf, zbuf, sem_idx, sem_tile, sem_row, sem_z):
    i = pl.program_id(0)
    n = pl.num_programs(0)
    slot = lax.rem(i, 2)
    tb = lax.rem(i, N_TILE_BUF)
    tile_rows = tm * nch
    blk_rows = MOE_BLOCK * nch

    def idx_fetch(step, s):
        return pltpu.make_async_copy(dest_hbm.at[step], dsm.at[s], sem_idx.at[s])

    def tile_load(step, b):
        row = pl.multiple_of(step * tile_rows, tile_rows)
        return pltpu.make_async_copy(h_hbm.at[pl.ds(row, tile_rows)], hbuf.at[b], sem_tile.at[b])

    def rows_done(s):
        return pltpu.make_async_copy(hbuf.at[0], xs_hbm.at[pl.ds(0, tile_rows)], sem_row.at[s])

    @pl.when(i == 0)
    def _():
        idx_fetch(0, 0).start()
        tile_load(0, 0).start()

    idx_fetch(i, slot).wait()
    tile_load(i, tb).wait()

    @pl.when(i + 1 < n)
    def _():
        idx_fetch(i + 1, 1 - slot).start()
        tile_load(i + 1, lax.rem(i + 1, N_TILE_BUF)).start()

    for r in range(tm):
        for k in range(TOP_K):
            _token_copy(nch, hbuf.at[tb], r, xs_hbm, dsm[slot, r * TOP_K + k],
                        sem_row.at[slot]).start(priority=k % 2)

    @pl.when(i > 0)
    def _():
        for _ in range(TOP_K):
            rows_done(1 - slot).wait()

    @pl.when(i == n - 1)
    def _():
        for _ in range(TOP_K):
            rows_done(slot).wait()
        zbuf[...] = jnp.zeros_like(zbuf)

        def per_expert(e, carry):
            start = zstart_ref[e]
            num = znum_ref[e]

            def zissue(r, c2):
                _token_copy(nch, zbuf, 0, xs_hbm, start + r, sem_z).start()
                return c2

            def zwait(r, c2):
                _token_copy(nch, zbuf, 0, xs_hbm, 0, sem_z).wait()
                return c2

            lax.fori_loop(0, num, zissue, 0)
            lax.fori_loop(0, num, zwait, 0)
            return carry

        lax.fori_loop(0, n_exp, per_expert, 0)

        def blk_copy(bi):
            row = pl.multiple_of(bi * blk_rows, blk_rows)
            return pltpu.make_async_copy(zbuf, xs_hbm.at[pl.ds(row, blk_rows)], sem_z)

        def bissue(bi, carry):
            blk_copy(bi).start()
            return carry

        def bwait(bi, carry):
            blk_copy(bi).wait()
            return carry

        lax.fori_loop(nused_ref[0], n_blocks, bissue, 0)
        lax.fori_loop(nused_ref[0], n_blocks, bwait, 0)


def _dispatch(zstart, znum, n_used, dest, h1_tt, nch, n_slots, tm):
    t = h1_tt.shape[0] // nch
    n_exp = zstart.shape[0]
    dest2 = dest.reshape(t // tm, tm * TOP_K)
    return pl.pallas_call(
        functools.partial(_dispatch_kernel, tm, nch, n_exp, n_slots // MOE_BLOCK),
        grid_spec=pltpu.PrefetchScalarGridSpec(
            num_scalar_prefetch=3,
            grid=(t // tm,),
            in_specs=[pl.BlockSpec(memory_space=pl.ANY), pl.BlockSpec(memory_space=pl.ANY)],
            out_specs=pl.BlockSpec(memory_space=pl.ANY),
            scratch_shapes=[pltpu.SMEM((2, tm * TOP_K), I32),
                            pltpu.VMEM((N_TILE_BUF, tm * nch, LANES), F32),
                            pltpu.VMEM((MOE_BLOCK * nch, LANES), F32),
                            pltpu.SemaphoreType.DMA((2,)), pltpu.SemaphoreType.DMA((N_TILE_BUF,)),
                            pltpu.SemaphoreType.DMA((2,)), pltpu.SemaphoreType.DMA],
        ),
        out_shape=jax.ShapeDtypeStruct((n_slots * nch, LANES), F32),
        compiler_params=_cparams(1),
        name="moe_dispatch",
    )(zstart, znum, n_used, dest2, h1_tt)


def _expert_kernel(d_ff, nch, be_ref, nused_ref, xs_ref, x_ref, wgu_ref, bgu_ref, wd_ref, bd_ref, y_ref,
                   wgu_q, wd_q, sgu_ref, sd_ref):
    i = pl.program_id(0)
    live = i < nused_ref[0]
    prev = be_ref[jnp.maximum(i - 1, 0)]

    def amax(v):
        return jnp.max(jnp.max(jnp.abs(v), axis=0, keepdims=True), axis=1, keepdims=True)

    def quant(v):
        m = jnp.maximum(amax(v), 1e-30)
        return (v * (FP8_PEAK / m)).astype(FP8), m * (1.0 / FP8_PEAK)

    @pl.when(jnp.logical_and(live, jnp.logical_or(i == 0, be_ref[i] != prev)))
    def _():
        wgu_q[...], sgu_ref[...] = quant(wgu_ref[0])
        wd_q[...], sd_ref[...] = quant(wd_ref[0])

    @pl.when(live)
    def _():
        x = (_load_token_tiles(x_ref, MOE_BLOCK, nch) * xs_ref[0:1, 0:1]).astype(FP8)
        gu = (jnp.dot(x, wgu_q[...], preferred_element_type=F32) * (xs_ref[0:1, 1:2] * sgu_ref[...])
              + bgu_ref[0])
        gu = gu.astype(BF16)
        gate = jnp.minimum(gu[:, :d_ff], SWIGLU_LIMIT)
        up = jnp.clip(gu[:, d_ff:], -SWIGLU_LIMIT, SWIGLU_LIMIT)
        act = gate * jax.nn.sigmoid(SWIGLU_ALPHA * gate) * (up + 1.0)
        a = (act * ACT_SCALE).astype(FP8)
        y = jnp.dot(a, wd_q[...], preferred_element_type=F32) * (sd_ref[...] * (1.0 / ACT_SCALE)) + bd_ref[0]
        _store_token_tiles(y_ref, y)

    @pl.when(i >= nused_ref[0])
    def _():
        y_ref[...] = jnp.zeros_like(y_ref)


def _experts(block_e, n_used, xs_tt, x_bound, layer, wgu_all, bgu, wd_all, bd):
    depth, n_exp, d, two_ff = wgu_all.shape
    x_scale = FP8_PEAK / jnp.maximum(x_bound, 1e-30)
    x_scales = jnp.zeros((1, LANES), F32).at[0, 0].set(x_scale).at[0, 1].set(1.0 / x_scale)
    nch = d // LANES
    n_slots = xs_tt.shape[0] // nch
    d_ff = two_ff // 2
    nb = n_slots // MOE_BLOCK
    live = lambda i, be, nu: jnp.minimum(i, nu[0] - 1)
    wgu = wgu_all.reshape(depth * n_exp, d, two_ff)
    wd = wd_all.reshape(depth * n_exp, d_ff, d)
    w_row = lambda i, be, nu: (layer * n_exp + be[live(i, be, nu)], 0, 0)
    return pl.pallas_call(
        functools.partial(_expert_kernel, d_ff, nch),
        grid_spec=pltpu.PrefetchScalarGridSpec(
            num_scalar_prefetch=2,
            grid=(nb,),
            in_specs=[
                pl.BlockSpec((1, LANES), lambda i, be, nu: (0, 0)),
                pl.BlockSpec((MOE_BLOCK * nch, LANES), lambda i, be, nu: (live(i, be, nu), 0)),
                pl.BlockSpec((1, d, two_ff), w_row),
                pl.BlockSpec((1, 1, two_ff), lambda i, be, nu: (be[live(i, be, nu)], 0, 0)),
                pl.BlockSpec((1, d_ff, d), w_row),
                pl.BlockSpec((1, 1, d), lambda i, be, nu: (be[live(i, be, nu)], 0, 0)),
            ],
            out_specs=pl.BlockSpec((MOE_BLOCK * nch, LANES), lambda i, be, nu: (i, 0)),
            scratch_shapes=[pltpu.VMEM((d, two_ff), FP8), pltpu.VMEM((d_ff, d), FP8),
                            pltpu.VMEM((1, 1), F32), pltpu.VMEM((1, 1), F32)],
        ),
        out_shape=jax.ShapeDtypeStruct((n_slots * nch, LANES), F32),
        compiler_params=_cparams(1, VMEM_LIMIT_EXPERTS),
        name="moe_experts",
    )(block_e, n_used, x_scales, xs_tt, wgu, bgu.reshape(n_exp, 1, two_ff), wd,
      bd.reshape(n_exp, 1, d))


def _combine_kernel(tm, nch, alpha, step0, with_bf16, dest_hbm, h1_ref, wt_ref, g_ref, b_ref, y_hbm,
                    *rest):
    if with_bf16:
        o_ref, ob16_ref, dsm, ybuf, sem_idx, sem = rest
    else:
        o_ref, dsm, ybuf, sem_idx, sem = rest
    i = pl.program_id(0)
    n = pl.num_programs(0)
    slot = lax.rem(i, 2)

    def idx_fetch(step, s):
        return pltpu.make_async_copy(dest_hbm.at[step0 + step], dsm.at[s], sem_idx.at[s])

    def gather(s):
        for r in range(tm):
            for k in range(TOP_K):
                _token_copy(nch, y_hbm, dsm[s, r * TOP_K + k], ybuf.at[s, k], r,
                            sem.at[s]).start(priority=k % 2)

    @pl.when(i == 0)
    def _():
        first = idx_fetch(0, 0)
        first.start()
        first.wait()
        gather(0)
        idx_fetch(1, 1).start()

    @pl.when(i + 1 < n)
    def _():
        idx_fetch(i + 1, 1 - slot).wait()
        gather(1 - slot)

    @pl.when(i + 2 < n)
    def _():
        idx_fetch(i + 2, slot).start()

    for k in range(TOP_K):
        pltpu.make_async_copy(y_hbm.at[pl.ds(0, tm * nch)], ybuf.at[slot, k], sem.at[slot]).wait()
    wt = wt_ref[...]
    f = wt[:, 0:1] * _load_token_tiles(ybuf.at[slot, 0], tm, nch)
    for k in range(1, TOP_K):
        f = f + wt[:, k:k + 1] * _load_token_tiles(ybuf.at[slot, k], tm, nch)
    h1 = _load_token_tiles(h1_ref, tm, nch)
    out = _layer_norm(alpha * h1 + f, g_ref[...], b_ref[...])
    o_ref[...] = out
    if with_bf16:
        ob16_ref[...] = out.astype(BF16)


def _combine(alpha, dest, h1_tt, wt128, g, b, y_tt, tm, tok0, ntok, with_bf16):
    d = g.shape[0]
    nch = d // LANES
    t = h1_tt.shape[0] // nch
    dest2 = dest.reshape(t // tm, tm * TOP_K)
    step0 = tok0 // tm
    nsteps = ntok // tm
    assert nsteps >= 2 and tok0 % tm == 0 and ntok % tm == 0
    vec = pl.BlockSpec((1, d), lambda i: (0, 0))
    out_specs = [pl.BlockSpec((tm, d), lambda i: (i, 0))]
    out_shape = [jax.ShapeDtypeStruct((ntok, d), F32)]
    if with_bf16:
        out_specs.append(pl.BlockSpec((tm, d), lambda i: (i, 0)))
        out_shape.append(jax.ShapeDtypeStruct((ntok, d), BF16))
    return pl.pallas_call(
        functools.partial(_combine_kernel, tm, nch, alpha, step0, with_bf16),
        grid=(nsteps,),
        in_specs=[pl.BlockSpec(memory_space=pl.ANY),
                  pl.BlockSpec((tm * nch, LANES), lambda i: (step0 + i, 0)),
                  pl.BlockSpec((tm, LANES), lambda i: (step0 + i, 0)), vec, vec,
                  pl.BlockSpec(memory_space=pl.ANY)],
        out_specs=out_specs,
        out_shape=out_shape,
        scratch_shapes=[pltpu.SMEM((2, tm * TOP_K), I32),
                        pltpu.VMEM((2, TOP_K, tm * nch, LANES), F32),
                        pltpu.SemaphoreType.DMA((2,)), pltpu.SemaphoreType.DMA((2,))],
        compiler_params=_cparams(1),
        name="moe_combine",
    )(dest2, h1_tt, wt128, g.reshape(1, d), b.reshape(1, d), y_tt)


def _route_plan(idx, rank, counts, n_blocks):
    padded = (counts + MOE_BLOCK - 1) // MOE_BLOCK * MOE_BLOCK
    pad_ends = jnp.cumsum(padded)
    pad_starts = pad_ends - padded
    dest = pad_starts[idx] + rank
    block_start = jnp.arange(n_blocks, dtype=I32) * MOE_BLOCK
    n_exp = counts.shape[0]
    block_e = jnp.minimum(jnp.sum(block_start[:, None] >= pad_ends[None, :], axis=1), n_exp - 1).astype(I32)
    n_used = (pad_ends[-1] // MOE_BLOCK).astype(I32).reshape(1)
    return dest.astype(I32), block_e, n_used, (pad_starts + counts).astype(I32), (padded - counts).astype(I32)


def _tile(n, pref):
    t = min(pref, n)
    while n % t:
        t //= 2
    return t


def kernel(x_prompt, x_sample, ln_in_g, ln_in_b, w_in, conv_w, conv_b, filt_w1, filt_b1, filt_freq1, filt_w2, filt_b2, filt_freq2, filt_w3, filt_decay, hy_bias, w_hy_out, w_fn_out, w_o, ln1_g, ln1_b, w_router, b_router, w_gu, b_gu, w_down, b_down, ln2_g, ln2_b):
    bp, lp, d = x_prompt.shape
    bs, ls, _ = x_sample.shape
    depth = w_in.shape[0]
    hy = hy_bias.shape[-1]
    fn = w_fn_out.shape[1]
    n_exp = w_router.shape[-1]
    tp, ts = bp * lp, bs * ls
    t = tp + ts
    alpha = (2 * depth) ** 0.25
    trunks = ((bp, lp), (bs, ls))
    n_blocks = -(-(t * TOP_K) // MOE_BLOCK) + n_exp
    n_slots = n_blocks * MOE_BLOCK

    tm_ln = _tile(math.gcd(tp, ts), 1024)
    tm_tok = _tile(math.gcd(lp, ls), 512)
    tm_proj = _tile(math.gcd(lp, ls), 1024)
    tm_row = _tile(math.gcd(tp, ts), 256)

    h, hb = _ln_in(x_prompt.reshape(tp, d), x_sample.reshape(ts, d), ln_in_g, ln_in_b, tm_ln)
    out = None
    for l in range(depth):
        x0, vv_a, vv_b, zf_a, zf_b, gates = _proj(hb, w_in[l].astype(BF16), conv_w[l], conv_b[l],
                                                  hy, fn, tp, lp, ls, tm_proj)
        filt_args = (filt_w1[l], filt_b1[l], filt_freq1[l], filt_w2[l], filt_b2[l], filt_freq2[l],
                     filt_w3[l], filt_decay[l])
        vvs, zfs = (vv_a, vv_b), (zf_a, zf_b)
        convs = [_hyena_conv(v, b, s, filt_args, 16) for v, (b, s) in zip(vvs, trunks)]
        yfns = [_fnet_mix(z, b, s, 16) for z, (b, s) in zip(zfs, trunks)]
        h1, idx128, wt128, rank128, cnt128 = _merge(
            alpha, convs, vvs, yfns, x0, gates, h, hy_bias[l], w_hy_out[l].astype(BF16),
            w_fn_out[l].astype(BF16), w_o[l].astype(BF16), ln1_g[l], ln1_b[l], w_router[l],
            b_router[l], tm_tok)
        dest, block_e, n_used, zstart, znum = _route_plan(
            idx128[:, :TOP_K], rank128[:, :TOP_K], cnt128[0, :n_exp], n_blocks)
        xs = _dispatch(zstart, znum, n_used, dest, h1, d // LANES, n_slots, tm_row)
        x_bound = math.sqrt(d) * jnp.max(jnp.abs(ln1_g[l])) + jnp.max(jnp.abs(ln1_b[l]))
        y = _experts(block_e, n_used, xs, x_bound, l, w_gu, b_gu[l], w_down, b_down[l])
        comb = functools.partial(_combine, alpha, dest, h1, wt128, ln2_g[l], ln2_b[l], y, tm_row)
        if l + 1 < depth:
            h, hb = comb(0, t, True)
        else:
            out = (comb(0, tp, False)[0], comb(tp, ts, False)[0])
    return (out[0].reshape(bp, lp, d), out[1].reshape(bs, ls, d))
```

```python
import functools
import math

import jax
import jax.numpy as jnp
from jax import lax
from jax.experimental import pallas as pl
from jax.experimental.pallas import tpu as pltpu

F32 = jnp.float32
BF16 = jnp.bfloat16
I32 = jnp.int32
FP8 = jnp.float8_e4m3fn
FP8_PEAK = 256.0

TOP_K = 4
FN_GROUPS = 4
MOE_BLOCK = 512
FILTER_BANDS = 16
MOD_SHIFT = 0.05
SWIGLU_LIMIT = 7.0
SWIGLU_ALPHA = 1.702
ACT_SCALE = 4.0
LN_EPS = 1e-5
LANES = 128
SUB = 8
DFT_RADIX = 128
FNET_MINOR = 64
VMEM_LIMIT = 48 * 1024 * 1024
VMEM_LIMIT_EXPERTS = 56 * 1024 * 1024


def _cparams(ndim, vmem=VMEM_LIMIT):
    return pltpu.CompilerParams(dimension_semantics=("arbitrary",) * ndim,
                                vmem_limit_bytes=vmem)


def _store_token_tiles(ref, x):
    rows, d = x.shape
    nch = d // LANES
    for c in range(nch):
        ref[pl.ds(c, rows, stride=nch), :] = x[:, c * LANES:(c + 1) * LANES]


def _load_token_tiles(ref, rows, nch):
    return jnp.concatenate([ref[pl.ds(c, rows, stride=nch), :] for c in range(nch)], axis=1)


def _layer_norm(x, g, b):
    mu = jnp.mean(x, axis=-1, keepdims=True)
    xc = x - mu
    var = jnp.mean(xc * xc, axis=-1, keepdims=True)
    return xc * lax.rsqrt(var + LN_EPS) * g + b


def _ln_in_kernel(n_first, xa_ref, xb_ref, g_ref, b_ref, h_ref, hb_ref):
    i = pl.program_id(0)

    def emit(x_ref):
        y = _layer_norm(x_ref[...], g_ref[...], b_ref[...])
        h_ref[...] = y
        hb_ref[...] = y.astype(BF16)

    @pl.when(i < n_first)
    def _():
        emit(xa_ref)

    @pl.when(i >= n_first)
    def _():
        emit(xb_ref)


def _ln_in(xa, xb, g, b, tm):
    ta, d = xa.shape
    tb = xb.shape[0]
    na, nb = ta // tm, tb // tm
    t = ta + tb
    return pl.pallas_call(
        functools.partial(_ln_in_kernel, na),
        grid=(na + nb,),
        in_specs=[
            pl.BlockSpec((tm, d), lambda i: (jnp.minimum(i, na - 1), 0)),
            pl.BlockSpec((tm, d), lambda i: (jnp.maximum(i - na, 0), 0)),
            pl.BlockSpec((1, d), lambda i: (0, 0)),
            pl.BlockSpec((1, d), lambda i: (0, 0)),
        ],
        out_specs=[pl.BlockSpec((tm, d), lambda i: (i, 0)),
                   pl.BlockSpec((tm, d), lambda i: (i, 0))],
        out_shape=[jax.ShapeDtypeStruct((t, d), F32), jax.ShapeDtypeStruct((t, d), BF16)],
        compiler_params=_cparams(1),
        name="ln_in",
    )(xa, xb, g.reshape(1, d), b.reshape(1, d))


HALO = 16


def _proj_kernel(tm, t_first, l_first, l_second, hy, fn, x_ref, xp_ref, xn_ref, w_ref, cw_ref,
                 cb_ref, x0_ref, vva_ref, vvb_ref, zfa_ref, zfb_ref, gate_ref):
    i = pl.program_id(0)
    r0 = i * tm
    seq = jnp.where(r0 < t_first, l_first, l_second)
    is_first = lax.rem(r0, seq) == 0
    is_last = lax.rem(r0 + tm, seq) == 0
    x = x_ref[...]
    x_ext = jnp.concatenate([xp_ref[...], x, xn_ref[...]], axis=0)
    row = lax.broadcasted_iota(I32, (tm, 1), 0)
    kill_prev = jnp.logical_and(row == 0, is_first)
    kill_next = jnp.logical_and(row == tm - 1, is_last)

    def conv_chunk(c):
        lo, hi = c * hy, (c + 1) * hy
        z = jnp.dot(x_ext, w_ref[:, lo:hi], preferred_element_type=F32)
        um1 = jnp.where(kill_prev, 0.0, z[HALO - 1:HALO - 1 + tm])
        up1 = jnp.where(kill_next, 0.0, z[HALO + 1:HALO + 1 + tm])
        return (um1 * cw_ref[0:1, lo:hi] + z[HALO:HALO + tm] * cw_ref[1:2, lo:hi]
                + up1 * cw_ref[2:3, lo:hi] + cb_ref[:, lo:hi])

    x0_ref[...] = conv_chunk(0).astype(BF16)
    vv = (conv_chunk(2) * conv_chunk(1)).astype(BF16)
    zfn = jnp.dot(x, w_ref[:, 3 * hy:3 * hy + fn], preferred_element_type=F32).astype(BF16)
    g = jnp.dot(x, w_ref[:, 3 * hy + fn:], preferred_element_type=F32)
    gate_ref[...] = jax.nn.sigmoid(g).astype(BF16)

    @pl.when(r0 < t_first)
    def _():
        vva_ref[...] = vv
        zfa_ref[...] = zfn

    @pl.when(r0 >= t_first)
    def _():
        vvb_ref[...] = vv
        zfb_ref[...] = zfn


def _ln_bound(g, b):
    return math.sqrt(g.shape[-1]) * jnp.max(jnp.abs(g)) + jnp.max(jnp.abs(b))


def _proj(hb, w_in, conv_w, conv_b, hy, fn, t_first, l_first, l_second, tm):
    t, d = hb.shape
    n = w_in.shape[1]
    ng = n - 3 * hy - fn
    nh = tm // HALO
    last = t // HALO - 1
    row = lambda c: pl.BlockSpec((tm, c), lambda i: (i, 0))
    na = t_first // tm
    row_a = lambda c: pl.BlockSpec((tm, c), lambda i: (jnp.minimum(i, na - 1), 0))
    row_b = lambda c: pl.BlockSpec((tm, c), lambda i: (jnp.maximum(i - na, 0), 0))
    bf = lambda r, c: jax.ShapeDtypeStruct((r, c), BF16)
    return pl.pallas_call(
        functools.partial(_proj_kernel, tm, t_first, l_first, l_second, hy, fn),
        grid=(t // tm,),
        in_specs=[
            row(d),
            pl.BlockSpec((HALO, d), lambda i: (jnp.maximum(i * nh - 1, 0), 0)),
            pl.BlockSpec((HALO, d), lambda i: (jnp.minimum((i + 1) * nh, last), 0)),
            pl.BlockSpec((d, n), lambda i: (0, 0)),
            pl.BlockSpec((3, 3 * hy), lambda i: (0, 0)),
            pl.BlockSpec((1, 3 * hy), lambda i: (0, 0)),
        ],
        out_specs=[row(hy), row_a(hy), row_b(hy), row_a(fn), row_b(fn), row(ng)],
        out_shape=[bf(t, hy), bf(t_first, hy), bf(t - t_first, hy),
                   bf(t_first, fn), bf(t - t_first, fn), bf(t, ng)],
        compiler_params=_cparams(1),
        name="proj_hyena_pre",
    )(hb, hb, hb, w_in, conv_w, conv_b.reshape(1, 3 * hy))


def _filter_kernel(tr, seq, w1t_ref, w1c_ref, w1s_ref, b1_ref, f1_ref, w2t_ref, b2_ref, f2_ref, w3_ref,
                   dec_ref, o_ref):
    i = pl.program_id(0)

    def tap(n):
        return jnp.where(n < seq, n, 2 * seq - n).astype(F32)

    n_l = i * tr + lax.broadcasted_iota(I32, (1, tr), 1)
    m_l = tap(n_l)
    band = lax.broadcasted_iota(I32, (FILTER_BANDS, 1), 0).astype(F32)
    freqs = 1e-4 + band * ((FILTER_BANDS - 1 - 1e-4) / (FILTER_BANDS - 1))
    fa = freqs * ((2.0 * math.pi / seq) * m_l)
    hi = lax.Precision.HIGHEST
    pre = (w1t_ref[...] * (m_l * (1.0 / (seq - 1)))
           + jnp.dot(w1c_ref[...], jnp.cos(fa), precision=hi, preferred_element_type=F32)
           - jnp.dot(w1s_ref[...], jnp.sin(fa), precision=hi, preferred_element_type=F32))
    h = jnp.sin(f1_ref[...] * (pre + b1_ref[...]))
    h = jnp.sin(f2_ref[...] * (jnp.dot(w2t_ref[...], h, precision=hi, preferred_element_type=F32)
                               + b2_ref[...]))
    out = lax.dot_general(h, w3_ref[...], (((0,), (0,)), ((), ())), precision=hi,
                          preferred_element_type=F32)
    n_s = i * tr + lax.broadcasted_iota(I32, (tr, 1), 0)
    t_s = tap(n_s) * (1.0 / (seq - 1))
    out = out * (jnp.exp(-t_s * jnp.abs(dec_ref[...])) + MOD_SHIFT)
    o_ref[...] = jnp.where(n_s == seq, 0.0, out)


def _filter(seq, w1, b1, f1, w2, b2, f2, w3, dec, tr):
    hid = w1.shape[1]
    hy = w3.shape[1] // 2
    nfwd = seq // tr
    side = lambda i: (0, jnp.where(i >= nfwd, 1, 0))
    full = lambda a: pl.BlockSpec(a.shape, lambda i: (0, 0))
    col = lambda v: v.reshape(hid, 1)
    small = (w1[0:1].T, w1[1:1 + FILTER_BANDS].T, w1[1 + FILTER_BANDS:].T, col(b1), col(f1),
             w2.T, col(b2), col(f2))
    return pl.pallas_call(
        functools.partial(_filter_kernel, tr, seq),
        grid=(2 * seq // tr,),
        in_specs=[full(a) for a in small] + [pl.BlockSpec((hid, hy), side),
                                             pl.BlockSpec((1, hy), side)],
        out_specs=pl.BlockSpec((tr, hy), lambda i: (i, 0)),
        out_shape=jax.ShapeDtypeStruct((2 * seq, hy), F32),
        compiler_params=_cparams(1),
        name="hyena_filter",
    )(*small, w3, dec.reshape(1, 2 * hy))


def _blm_kernel(tj, shared, m_ref, x_ref, o_ref):
    for jj in range(tj):
        m = m_ref[0 if shared else jj]
        o_ref[0, jj] = jnp.dot(m, x_ref[0, jj], preferred_element_type=F32).astype(o_ref.dtype)


def _blm(m, x, tj, out_dtype=BF16):
    g, j, k, c = x.shape
    jm, mr, _ = m.shape
    shared = jm == 1
    tj = min(tj, j)
    m_spec = (pl.BlockSpec((1, mr, k), lambda jb, gb: (0, 0, 0)) if shared
              else pl.BlockSpec((tj, mr, k), lambda jb, gb: (jb, 0, 0)))
    return pl.pallas_call(
        functools.partial(_blm_kernel, tj, shared),
        grid=(j // tj, g),
        in_specs=[m_spec, pl.BlockSpec((1, tj, k, c), lambda jb, gb: (gb, jb, 0, 0))],
        out_specs=pl.BlockSpec((1, tj, mr, c), lambda jb, gb: (gb, jb, 0, 0)),
        out_shape=jax.ShapeDtypeStruct((g, j, mr, c), out_dtype),
        compiler_params=_cparams(2),
        name="dft_stage",
    )(m, x)


def _fnet_s1_kernel(tj, fn, m_ref, r_ref, x_ref, o_ref):
    for jj in range(tj):
        z = jnp.dot(x_ref[0, jj], r_ref[...], preferred_element_type=F32)
        zs = jnp.concatenate([z[:, :fn], z[:, fn:]], axis=0).astype(BF16)
        o_ref[0, jj] = jnp.dot(m_ref[jj], zs, preferred_element_type=F32).astype(o_ref.dtype)


def _fnet_s1(m, r, x, tj):
    g, j, k, fn = x.shape
    _, mr, k2 = m.shape
    tj = min(tj, j)
    return pl.pallas_call(
        functools.partial(_fnet_s1_kernel, tj, fn),
        grid=(j // tj, g),
        in_specs=[pl.BlockSpec((tj, mr, k2), lambda jb, gb: (jb, 0, 0)),
                  pl.BlockSpec(r.shape, lambda jb, gb: (0, 0)),
                  pl.BlockSpec((1, tj, k, fn), lambda jb, gb: (gb, jb, 0, 0))],
        out_specs=pl.BlockSpec((1, tj, mr, fn), lambda jb, gb: (gb, jb, 0, 0)),
        out_shape=jax.ShapeDtypeStruct((g, j, mr, fn), BF16),
        compiler_params=_cparams(2),
        name="fnet_stage1",
    )(m, r, x)


def _hyena_mid_kernel(tk, n2, f_ref, kf_ref, gi_ref, x_ref, o_ref):
    for kk in range(tk):
        a = jnp.dot(f_ref[...], x_ref[0, kk], preferred_element_type=F32)
        ar, ai = a[:n2], a[n2:]
        kr = kf_ref[kk, :n2].astype(F32)
        ki = kf_ref[kk, n2:].astype(F32)
        prod = jnp.concatenate([ar * kr - ai * ki, ar * ki + ai * kr], axis=0).astype(BF16)
        o_ref[0, kk] = jnp.dot(gi_ref[kk], prod, preferred_element_type=F32).astype(o_ref.dtype)


def _hyena_mid(f2s, kf, ginv, x, tk):
    g, n1, r, c = x.shape
    n2 = r // 2
    tk = min(tk, n1)
    return pl.pallas_call(
        functools.partial(_hyena_mid_kernel, tk, n2),
        grid=(n1 // tk, g),
        in_specs=[pl.BlockSpec((r, r), lambda kb, gb: (0, 0)),
                  pl.BlockSpec((tk, r, c), lambda kb, gb: (kb, 0, 0)),
                  pl.BlockSpec((tk, r, r), lambda kb, gb: (kb, 0, 0)),
                  pl.BlockSpec((1, tk, r, c), lambda kb, gb: (gb, kb, 0, 0))],
        out_specs=pl.BlockSpec((1, tk, r, c), lambda kb, gb: (gb, kb, 0, 0)),
        out_shape=jax.ShapeDtypeStruct(x.shape, BF16),
        compiler_params=_cparams(2),
        name="hyena_mid",
    )(f2s, kf, ginv, x)


def _cis(num, den, sign):
    ang = (2.0 * math.pi / den) * lax.rem(num, den).astype(F32)
    return jnp.cos(ang), sign * jnp.sin(ang)


def _stack_complex(mr, mi):
    top = jnp.concatenate([mr, -mi], axis=-1)
    bot = jnp.concatenate([mi, mr], axis=-1)
    return jnp.concatenate([top, bot], axis=-2)


def _iota(shape, axis):
    return lax.broadcasted_iota(I32, shape, axis)


def _hyena_mats(n, n1, n2):
    n1h = n1 // 2
    sh = (n2, n1, n1)
    gr, gi = _cis(_iota(sh, 1) * (n2 * _iota(sh, 2) + _iota(sh, 0)), n, -1.0)
    g_data = _stack_complex(gr[:, :, :n1h], gi[:, :, :n1h]).astype(BF16)
    g_filt = jnp.concatenate([gr, gi], axis=1).astype(BF16)
    sh = (n2, n2)
    fr, fi = _cis(_iota(sh, 0) * _iota(sh, 1), n2, -1.0)
    f2s = _stack_complex(fr, fi).astype(BF16)
    sh = (n1, n2, n2)
    ir, ii = _cis(_iota(sh, 1) * (_iota(sh, 0) + n1 * _iota(sh, 2)), n, 1.0)
    ginv = _stack_complex(ir, ii).astype(BF16)
    sh = (n1h, n1)
    br, bi = _cis(_iota(sh, 0) * _iota(sh, 1), n1, 1.0)
    f1inv = (_stack_complex(br, bi) * (1.0 / n)).astype(BF16)[None]
    return g_data, g_filt, f2s, ginv, f1inv


def _fnet_mats(seq, na, nb, fn):
    gd = fn // FN_GROUPS
    sh = (nb, na, na)
    gr, gi = _cis(_iota(sh, 1) * (nb * _iota(sh, 2) + _iota(sh, 0)), seq, -1.0)
    g1 = _stack_complex(gr, gi).astype(BF16)
    sh = (nb, nb)
    fr, fi = _cis(_iota(sh, 0) * _iota(sh, 1), nb, -1.0)
    scale = 1.0 / math.sqrt(seq * gd)
    f2re = (jnp.concatenate([fr, -fi], axis=-1) * scale).astype(BF16)[None]
    sh = (fn, fn)
    same = (_iota(sh, 0) // gd) == (_iota(sh, 1) // gd)
    cr, ci = _cis(_iota(sh, 0) * _iota(sh, 1), gd, -1.0)
    rmat = jnp.concatenate([jnp.where(same, cr, 0.0), jnp.where(same, ci, 0.0)], axis=1).astype(BF16)
    return g1, f2re, rmat


def _hyena_conv(vv, batch, seq, filt_args, tj):
    c = vv.shape[1]
    n = 2 * seq
    n1 = DFT_RADIX
    n2 = n // n1
    n1h = n1 // 2
    bp = batch // 2
    g_data, g_filt, f2s, ginv, f1inv = _hyena_mats(n, n1, n2)
    kt = _filter(seq, *filt_args, tr=min(512, seq))
    kt = kt.reshape(1, n1, n2, c).transpose(0, 2, 1, 3).astype(BF16)
    ks = _blm(g_filt, kt, tj)
    ks = ks.reshape(1, n2, 2, n1, c).transpose(0, 3, 2, 1, 4).reshape(1, n1, 2 * n2, c)
    kf = _blm(f2s[None], ks, tj)[0]
    x = vv.reshape(bp, 2, n1h, n2, c).transpose(0, 3, 1, 2, 4).reshape(bp, n2, n1, c)
    a = _blm(g_data, x, tj)
    a = a.reshape(bp, n2, 2, n1, c).transpose(0, 3, 2, 1, 4).reshape(bp, n1, 2 * n2, c)
    z = _hyena_mid(f2s, kf, ginv, a, tj // 2)
    z = z.reshape(bp, n1, 2, n2, c).transpose(0, 3, 2, 1, 4).reshape(bp, n2, 2 * n1, c)
    y = _blm(f1inv, z, tj)
    y = y.reshape(bp, n2, 2, n1h, c).transpose(0, 2, 3, 1, 4)
    return y.reshape(batch * seq, c)


def _fnet_mix(u, batch, seq, tj):
    fn = u.shape[1]
    nb = FNET_MINOR
    na = seq // nb
    g1, f2re, rmat = _fnet_mats(seq, na, nb, fn)
    x = u.reshape(batch, na, nb, fn).transpose(0, 2, 1, 3)
    a = _fnet_s1(g1, rmat, x, tj)
    a = a.reshape(batch, nb, 2, na, fn).transpose(0, 3, 2, 1, 4).reshape(batch, na, 2 * nb, fn)
    y = _blm(f2re, a, tj)
    return y.transpose(0, 2, 1, 3).reshape(batch * seq, fn)


def _merge_kernel(alpha, d, n_exp, n_first, conva_ref, convb_ref, vva_ref, vvb_ref, yfna_ref, yfnb_ref,
                  x0_ref, gate_ref, h_ref, bias_ref,
                  why_ref, wfn_ref, wo_ref, g_ref, b_ref, wr_ref, br_ref,
                  h1_ref, idx_ref, wt_ref, rank_ref, cnt_ref, carry_ref):
    in_first = pl.program_id(0) < n_first

    @pl.when(pl.program_id(0) == 0)
    def _():
        carry_ref[...] = jnp.zeros_like(carry_ref)

    pick = lambda a_ref, b_ref: jnp.where(in_first, a_ref[...], b_ref[...])
    vv = pick(vva_ref, vvb_ref).astype(F32)
    y_hy = x0_ref[...].astype(F32) * (pick(conva_ref, convb_ref).astype(F32) + vv * bias_ref[...])
    a = jnp.dot(y_hy.astype(BF16), why_ref[...], preferred_element_type=F32)
    bfn = jnp.dot(pick(yfna_ref, yfnb_ref), wfn_ref[...], preferred_element_type=F32)
    gates = gate_ref[...].astype(F32)
    merged = gates[:, :d] * a + gates[:, d:] * bfn
    m = jnp.dot(merged.astype(BF16), wo_ref[...], preferred_element_type=F32)
    h1 = _layer_norm(alpha * h_ref[...] + m, g_ref[...], b_ref[...])
    _store_token_tiles(h1_ref, h1)
    logits = jnp.dot(h1.astype(BF16), wr_ref[...], preferred_element_type=F32) + br_ref[...]
    lane = lax.broadcasted_iota(I32, logits.shape, 1)
    lane_f = lane.astype(F32)
    neg = jnp.float32(-jnp.inf)
    logits = jnp.where(lane < n_exp, logits, neg)
    vals, idxs = [], []
    for _ in range(TOP_K):
        mx = jnp.max(logits, axis=-1, keepdims=True)
        ix = jnp.min(jnp.where(logits == mx, lane_f, float(LANES)), axis=-1, keepdims=True).astype(I32)
        vals.append(mx)
        idxs.append(ix)
        logits = jnp.where(lane == ix, neg, logits)
    exps = [jnp.exp(v - vals[0]) for v in vals]
    den = exps[0]
    for e in exps[1:]:
        den = den + e
    tm = lane.shape[0]
    onehot = jnp.zeros(lane.shape, F32)
    for ix in idxs:
        onehot = onehot + (lane == ix).astype(F32)
    tri = (lax.broadcasted_iota(I32, (tm, tm), 1) < lax.broadcasted_iota(I32, (tm, tm), 0)).astype(BF16)
    before = jnp.dot(tri, onehot.astype(BF16), preferred_element_type=F32) + carry_ref[...]
    idx_out = jnp.zeros(lane.shape, I32)
    wt_out = jnp.zeros(lane.shape, F32)
    rank_out = jnp.zeros(lane.shape, F32)
    for k in range(TOP_K):
        idx_out = jnp.where(lane == k, idxs[k], idx_out)
        wt_out = jnp.where(lane == k, exps[k] / den, wt_out)
        rk = jnp.sum(jnp.where(lane == idxs[k], before, 0.0), axis=-1, keepdims=True)
        rank_out = jnp.where(lane == k, rk, rank_out)
    idx_ref[...] = idx_out
    wt_ref[...] = wt_out
    rank_ref[...] = rank_out.astype(I32)
    total = carry_ref[...] + jnp.sum(onehot, axis=0, keepdims=True)
    carry_ref[...] = total
    cnt_ref[...] = total.astype(I32)


def _merge(alpha, convs, vvs, yfns, x0, gates, h, bias, why, wfn, wo, g, b, wr, br, tm):
    t, d = h.shape
    hy = x0.shape[1]
    fn = yfns[0].shape[1]
    n_exp = wr.shape[1]
    na = convs[0].shape[0] // tm
    wr_p = jnp.zeros((d, LANES), BF16).at[:, :n_exp].set(wr.astype(BF16))
    br_p = jnp.zeros((1, LANES), F32).at[0, :n_exp].set(br)
    row = lambda c: pl.BlockSpec((tm, c), lambda i: (i, 0))
    row_a = lambda c: pl.BlockSpec((tm, c), lambda i: (jnp.minimum(i, na - 1), 0))
    row_b = lambda c: pl.BlockSpec((tm, c), lambda i: (jnp.maximum(i - na, 0), 0))
    full = lambda a: pl.BlockSpec(a.shape, lambda i: (0,) * a.ndim)
    args = (convs[0], convs[1], vvs[0], vvs[1], yfns[0], yfns[1], x0, gates, h,
            bias.reshape(1, hy), why, wfn, wo, g.reshape(1, d), b.reshape(1, d), wr_p, br_p)
    in_specs = ([row_a(hy), row_b(hy), row_a(hy), row_b(hy), row_a(fn), row_b(fn),
                 row(hy), row(2 * d), row(d)] + [full(a) for a in args[9:]])
    return pl.pallas_call(
        functools.partial(_merge_kernel, alpha, d, n_exp, na),
        grid=(t // tm,),
        in_specs=in_specs,
        out_specs=[pl.BlockSpec((tm * (d // LANES), LANES), lambda i: (i, 0)), row(LANES), row(LANES),
                   row(LANES), pl.BlockSpec((1, LANES), lambda i: (0, 0))],
        out_shape=[jax.ShapeDtypeStruct((t * (d // LANES), LANES), F32),
                   jax.ShapeDtypeStruct((t, LANES), I32), jax.ShapeDtypeStruct((t, LANES), F32),
                   jax.ShapeDtypeStruct((t, LANES), I32), jax.ShapeDtypeStruct((1, LANES), I32)],
        scratch_shapes=[pltpu.VMEM((1, LANES), F32)],
        compiler_params=_cparams(1),
        name="merge_route",
    )(*args)


def _token_copy(nch, src_ref, src_tok, dst_ref, dst_tok, sem):
    rows = lambda tok: pl.ds(tok * nch if isinstance(tok, int) else pl.multiple_of(tok * nch, nch), nch)
    return pltpu.make_async_copy(src_ref.at[rows(src_tok)], dst_ref.at[rows(dst_tok)], sem)


N_TILE_BUF = 3


def _dispatch_kernel(tm, nch, n_exp, n_blocks, zstart_ref, znum_ref, nused_ref, dest_hbm, h_hbm, xs_hbm,
                     dsm, hbuf, zbuf, sem_idx, sem_tile, sem_row, sem_z):
    i = pl.program_id(0)
    n = pl.num_programs(0)
    slot = lax.rem(i, 2)
    tb = lax.rem(i, N_TILE_BUF)
    tile_rows = tm * nch
    blk_rows = MOE_BLOCK * nch

    def idx_fetch(step, s):
        return pltpu.make_async_copy(dest_hbm.at[step], dsm.at[s], sem_idx.at[s])

    def tile_load(step, b):
        row = pl.multiple_of(step * tile_rows, tile_rows)
        return pltpu.make_async_copy(h_hbm.at[pl.ds(row, tile_rows)], hbuf.at[b], sem_tile.at[b])

    def rows_done(s):
        return pltpu.make_async_copy(hbuf.at[0], xs_hbm.at[pl.ds(0, tile_rows)], sem_row.at[s])

    @pl.when(i == 0)
    def _():
        idx_fetch(0, 0).start()
        tile_load(0, 0).start()

    idx_fetch(i, slot).wait()
    tile_load(i, tb).wait()

    @pl.when(i + 1 < n)
    def _():
        idx_fetch(i + 1, 1 - slot).start()
        tile_load(i + 1, lax.rem(i + 1, N_TILE_BUF)).start()

    for r in range(tm):
        for k in range(TOP_K):
            _token_copy(nch, hbuf.at[tb], r, xs_hbm, dsm[slot, r * TOP_K + k],
                        sem_row.at[slot]).start(priority=k % 2)

    @pl.when(i > 0)
    def _():
        for _ in range(TOP_K):
            rows_done(1 - slot).wait()

    @pl.when(i == n - 1)
    def _():
        for _ in range(TOP_K):
            rows_done(slot).wait()
        zbuf[...] = jnp.zeros_like(zbuf)

        def per_expert(e, carry):
            start = zstart_ref[e]
            num = znum_ref[e]

            def zissue(r, c2):
                _token_copy(nch, zbuf, 0, xs_hbm, start + r, sem_z).start()
                return c2

            def zwait(r, c2):
                _token_copy(nch, zbuf, 0, xs_hbm, 0, sem_z).wait()
                return c2

            lax.fori_loop(0, num, zissue, 0)
            lax.fori_loop(0, num, zwait, 0)
            return carry

        lax.fori_loop(0, n_exp, per_expert, 0)

        def blk_copy(bi):
            row = pl.multiple_of(bi * blk_rows, blk_rows)
            return pltpu.make_async_copy(zbuf, xs_hbm.at[pl.ds(row, blk_rows)], sem_z)

        def bissue(bi, carry):
            blk_copy(bi).start()
            return carry

        def bwait(bi, carry):
            blk_copy(bi).wait()
            return carry

        lax.fori_loop(nused_ref[0], n_blocks, bissue, 0)
        lax.fori_loop(nused_ref[0], n_blocks, bwait, 0)


def _dispatch(zstart, znum, n_used, dest, h1_tt, nch, n_slots, tm):
    t = h1_tt.shape[0] // nch
    n_exp = zstart.shape[0]
    dest2 = dest.reshape(t // tm, tm * TOP_K)
    return pl.pallas_call(
        functools.partial(_dispatch_kernel, tm, nch, n_exp, n_slots // MOE_BLOCK),
        grid_spec=pltpu.PrefetchScalarGridSpec(
            num_scalar_prefetch=3,
            grid=(t // tm,),
            in_specs=[pl.BlockSpec(memory_space=pl.ANY), pl.BlockSpec(memory_space=pl.ANY)],
            out_specs=pl.BlockSpec(memory_space=pl.ANY),
            scratch_shapes=[pltpu.SMEM((2, tm * TOP_K), I32),
                            pltpu.VMEM((N_TILE_BUF, tm * nch, LANES), F32),
                            pltpu.VMEM((MOE_BLOCK * nch, LANES), F32),
                            pltpu.SemaphoreType.DMA((2,)), pltpu.SemaphoreType.DMA((N_TILE_BUF,)),
                            pltpu.SemaphoreType.DMA((2,)), pltpu.SemaphoreType.DMA],
        ),
        out_shape=jax.ShapeDtypeStruct((n_slots * nch, LANES), F32),
        compiler_params=_cparams(1),
        name="moe_dispatch",
    )(zstart, znum, n_used, dest2, h1_tt)


def _expert_kernel(d_ff, nch, be_ref, nused_ref, xs_ref, x_ref, wgu_ref, bgu_ref, wd_ref, bd_ref, y_ref,
                   wgu_q, wd_q, sgu_ref, sd_ref):
    i = pl.program_id(0)
    live = i < nused_ref[0]
    prev = be_ref[jnp.maximum(i - 1, 0)]

    def amax(v):
        return jnp.max(jnp.max(jnp.abs(v), axis=0, keepdims=True), axis=1, keepdims=True)

    def quant(v):
        m = jnp.maximum(amax(v), 1e-30)
        return (v * (FP8_PEAK / m)).astype(FP8), m * (1.0 / FP8_PEAK)

    @pl.when(jnp.logical_and(live, jnp.logical_or(i == 0, be_ref[i] != prev)))
    def _():
        wgu_q[...], sgu_ref[...] = quant(wgu_ref[0])
        wd_q[...], sd_ref[...] = quant(wd_ref[0])

    @pl.when(live)
    def _():
        x = (_load_token_tiles(x_ref, MOE_BLOCK, nch) * xs_ref[0:1, 0:1]).astype(FP8)
        gu = (jnp.dot(x, wgu_q[...], preferred_element_type=F32) * (xs_ref[0:1, 1:2] * sgu_ref[...])
              + bgu_ref[0])
        gu = gu.astype(BF16)
        gate = jnp.minimum(gu[:, :d_ff], SWIGLU_LIMIT)
        up = jnp.clip(gu[:, d_ff:], -SWIGLU_LIMIT, SWIGLU_LIMIT)
        act = gate * jax.nn.sigmoid(SWIGLU_ALPHA * gate) * (up + 1.0)
        a = (act * ACT_SCALE).astype(FP8)
        y = jnp.dot(a, wd_q[...], preferred_element_type=F32) * (sd_ref[...] * (1.0 / ACT_SCALE)) + bd_ref[0]
        _store_token_tiles(y_ref, y)

    @pl.when(i >= nused_ref[0])
    def _():
        y_ref[...] = jnp.zeros_like(y_ref)


def _experts(block_e, n_used, xs_tt, x_bound, layer, wgu_all, bgu, wd_all, bd):
    depth, n_exp, d, two_ff = wgu_all.shape
    x_scale = FP8_PEAK / jnp.maximum(x_bound, 1e-30)
    x_scales = jnp.zeros((1, LANES), F32).at[0, 0].set(x_scale).at[0, 1].set(1.0 / x_scale)
    nch = d // LANES
    n_slots = xs_tt.shape[0] // nch
    d_ff = two_ff // 2
    nb = n_slots // MOE_BLOCK
    live = lambda i, be, nu: jnp.minimum(i, nu[0] - 1)
    wgu = wgu_all.reshape(depth * n_exp, d, two_ff)
    wd = wd_all.reshape(depth * n_exp, d_ff, d)
    w_row = lambda i, be, nu: (layer * n_exp + be[live(i, be, nu)], 0, 0)
    return pl.pallas_call(
        functools.partial(_expert_kernel, d_ff, nch),
        grid_spec=pltpu.PrefetchScalarGridSpec(
            num_scalar_prefetch=2,
            grid=(nb,),
            in_specs=[
                pl.BlockSpec((1, LANES), lambda i, be, nu: (0, 0)),
                pl.BlockSpec((MOE_BLOCK * nch, LANES), lambda i, be, nu: (live(i, be, nu), 0)),
                pl.BlockSpec((1, d, two_ff), w_row),
                pl.BlockSpec((1, 1, two_ff), lambda i, be, nu: (be[live(i, be, nu)], 0, 0)),
                pl.BlockSpec((1, d_ff, d), w_row),
                pl.BlockSpec((1, 1, d), lambda i, be, nu: (be[live(i, be, nu)], 0, 0)),
            ],
            out_specs=pl.BlockSpec((MOE_BLOCK * nch, LANES), lambda i, be, nu: (i, 0)),
            scratch_shapes=[pltpu.VMEM((d, two_ff), FP8), pltpu.VMEM((d_ff, d), FP8),
                            pltpu.VMEM((1, 1), F32), pltpu.VMEM((1, 1), F32)],
        ),
        out_shape=jax.ShapeDtypeStruct((n_slots * nch, LANES), F32),
        compiler_params=_cparams(1, VMEM_LIMIT_EXPERTS),
        name="moe_experts",
    )(block_e, n_used, x_scales, xs_tt, wgu, bgu.reshape(n_exp, 1, two_ff), wd,
      bd.reshape(n_exp, 1, d))


def _combine_kernel(tm, nch, alpha, step0, with_bf16, dest_hbm, h1_ref, wt_ref, g_ref, b_ref, y_hbm,
                    *rest):
    if with_bf16:
        o_ref, ob16_ref, dsm, ybuf, sem_idx, sem = rest
    else:
        o_ref, dsm, ybuf, sem_idx, sem = rest
    i = pl.program_id(0)
    n = pl.num_programs(0)
    slot = lax.rem(i, 2)

    def idx_fetch(step, s):
        return pltpu.make_async_copy(dest_hbm.at[step0 + step], dsm.at[s], sem_idx.at[s])

    def gather(s):
        for r in range(tm):
            for k in range(TOP_K):
                _token_copy(nch, y_hbm, dsm[s, r * TOP_K + k], ybuf.at[s, k], r,
                            sem.at[s]).start(priority=k % 2)

    @pl.when(i == 0)
    def _():
        first = idx_fetch(0, 0)
        first.start()
        first.wait()
        gather(0)
        idx_fetch(1, 1).start()

    @pl.when(i + 1 < n)
    def _():
        idx_fetch(i + 1, 1 - slot).wait()
        gather(1 - slot)

    @pl.when(i + 2 < n)
    def _():
        idx_fetch(i + 2, slot).start()

    for k in range(TOP_K):
        pltpu.make_async_copy(y_hbm.at[pl.ds(0, tm * nch)], ybuf.at[slot, k], sem.at[slot]).wait()
    wt = wt_ref[...]
    f = wt[:, 0:1] * _load_token_tiles(ybuf.at[slot, 0], tm, nch)
    for k in range(1, TOP_K):
        f = f + wt[:, k:k + 1] * _load_token_tiles(ybuf.at[slot, k], tm, nch)
    h1 = _load_token_tiles(h1_ref, tm, nch)
    out = _layer_norm(alpha * h1 + f, g_ref[...], b_ref[...])
    o_ref[...] = out
    if with_bf16:
        ob16_ref[...] = out.astype(BF16)


def _combine(alpha, dest, h1_tt, wt128, g, b, y_tt, tm, tok0, ntok, with_bf16):
    d = g.shape[0]
    nch = d // LANES
    t = h1_tt.shape[0] // nch
    dest2 = dest.reshape(t // tm, tm * TOP_K)
    step0 = tok0 // tm
    nsteps = ntok // tm
    assert nsteps >= 2 and tok0 % tm == 0 and ntok % tm == 0
    vec = pl.BlockSpec((1, d), lambda i: (0, 0))
    out_specs = [pl.BlockSpec((tm, d), lambda i: (i, 0))]
    out_shape = [jax.ShapeDtypeStruct((ntok, d), F32)]
    if with_bf16:
        out_specs.append(pl.BlockSpec((tm, d), lambda i: (i, 0)))
        out_shape.append(jax.ShapeDtypeStruct((ntok, d), BF16))
    return pl.pallas_call(
        functools.partial(_combine_kernel, tm, nch, alpha, step0, with_bf16),
        grid=(nsteps,),
        in_specs=[pl.BlockSpec(memory_space=pl.ANY),
                  pl.BlockSpec((tm * nch, LANES), lambda i: (step0 + i, 0)),
                  pl.BlockSpec((tm, LANES), lambda i: (step0 + i, 0)), vec, vec,
                  pl.BlockSpec(memory_space=pl.ANY)],
        out_specs=out_specs,
        out_shape=out_shape,
        scratch_shapes=[pltpu.SMEM((2, tm * TOP_K), I32),
                        pltpu.VMEM((2, TOP_K, tm * nch, LANES), F32),
                        pltpu.SemaphoreType.DMA((2,)), pltpu.SemaphoreType.DMA((2,))],
        compiler_params=_cparams(1),
        name="moe_combine",
    )(dest2, h1_tt, wt128, g.reshape(1, d), b.reshape(1, d), y_tt)


def _route_plan(idx, rank, counts, n_blocks):
    padded = (counts + MOE_BLOCK - 1) // MOE_BLOCK * MOE_BLOCK
    pad_ends = jnp.cumsum(padded)
    pad_starts = pad_ends - padded
    dest = pad_starts[idx] + rank
    block_start = jnp.arange(n_blocks, dtype=I32) * MOE_BLOCK
    n_exp = counts.shape[0]
    block_e = jnp.minimum(jnp.sum(block_start[:, None] >= pad_ends[None, :], axis=1), n_exp - 1).astype(I32)
    n_used = (pad_ends[-1] // MOE_BLOCK).astype(I32).reshape(1)
    return dest.astype(I32), block_e, n_used, (pad_starts + counts).astype(I32), (padded - counts).astype(I32)


def _tile(n, pref):
    t = min(pref, n)
    while n % t:
        t //= 2
    return t


def kernel(x_prompt, x_sample, ln_in_g, ln_in_b, w_in, conv_w, conv_b, filt_w1, filt_b1, filt_freq1, filt_w2, filt_b2, filt_freq2, filt_w3, filt_decay, hy_bias, w_hy_out, w_fn_out, w_o, ln1_g, ln1_b, w_router, b_router, w_gu, b_gu, w_down, b_down, ln2_g, ln2_b):
    bp, lp, d = x_prompt.shape
    bs, ls, _ = x_sample.shape
    depth = w_in.shape[0]
    hy = hy_bias.shape[-1]
    fn = w_fn_out.shape[1]
    n_exp = w_router.shape[-1]
    tp, ts = bp * lp, bs * ls
    t = tp + ts
    alpha = (2 * depth) ** 0.25
    trunks = ((bp, lp), (bs, ls))
    n_blocks = -(-(t * TOP_K) // MOE_BLOCK) + n_exp
    n_slots = n_blocks * MOE_BLOCK

    tm_ln = _tile(math.gcd(tp, ts), 1024)
    tm_tok = _tile(math.gcd(lp, ls), 512)
    tm_proj = _tile(math.gcd(lp, ls), 1024)
    tm_row = _tile(math.gcd(tp, ts), 512)

    h, hb = _ln_in(x_prompt.reshape(tp, d), x_sample.reshape(ts, d), ln_in_g, ln_in_b, tm_ln)
    out = None
    for l in range(depth):
        x0, vv_a, vv_b, zf_a, zf_b, gates = _proj(hb, w_in[l].astype(BF16), conv_w[l], conv_b[l],
                                                  hy, fn, tp, lp, ls, tm_proj)
        filt_args = (filt_w1[l], filt_b1[l], filt_freq1[l], filt_w2[l], filt_b2[l], filt_freq2[l],
                     filt_w3[l], filt_decay[l])
        vvs, zfs = (vv_a, vv_b), (zf_a, zf_b)
        convs = [_hyena_conv(v, b, s, filt_args, 16) for v, (b, s) in zip(vvs, trunks)]
        yfns = [_fnet_mix(z, b, s, 16) for z, (b, s) in zip(zfs, trunks)]
        h1, idx128, wt128, rank128, cnt128 = _merge(
            alpha, convs, vvs, yfns, x0, gates, h, hy_bias[l], w_hy_out[l].astype(BF16),
            w_fn_out[l].astype(BF16), w_o[l].astype(BF16), ln1_g[l], ln1_b[l], w_router[l],
            b_router[l], tm_tok)
        dest, block_e, n_used, zstart, znum = _route_plan(
            idx128[:, :TOP_K], rank128[:, :TOP_K], cnt128[0, :n_exp], n_blocks)
        xs = _dispatch(zstart, znum, n_used, dest, h1, d // LANES, n_slots, tm_row)
        y = _experts(block_e, n_used, xs, _ln_bound(ln1_g[l], ln1_b[l]), l, w_gu, b_gu[l], w_down,
                     b_down[l])
        comb = functools.partial(_combine, alpha, dest, h1, wt128, ln2_g[l], ln2_b[l], y, tm_row)
        if l + 1 < depth:
            h, hb = comb(0, t, True)
        else:
            out = (comb(0, tp, False)[0], comb(tp, ts, False)[0])
    return (out[0].reshape(bp, lp, d), out[1].reshape(bs, ls, d))
```
